```python
import jax, jax.numpy as jnp
from jax import lax
import numpy as np

D_MODEL = 2048
BATCH = 2
SEQ = 8192
DEPTH = 4
DEC_BATCH = 16
DEC_SEQ = 32
PAST_LEN = 1024

CHUNK = 64
EPS = 1e-6
W_CONV = D_MODEL // 4
W_SSM = D_MODEL // 4
W_RET = D_MODEL // 4
W_POOL = D_MODEL - W_CONV - W_SSM - W_RET
CONV_K = 31
SSM_HEADDIM = 64
SSM_HEADS = W_SSM // SSM_HEADDIM
SSM_GROUPS = 2
SSM_STATE = 128
SSM_CONV_K = 4
SSM_XBC = W_SSM + 2 * SSM_GROUPS * SSM_STATE
RET_HEADS = 4
RET_VDIM = W_RET // RET_HEADS
RET_KDIM = RET_VDIM // 2
ROPE_BASE = 10000.0
POOL_WINDOWS = (2, 4, 8, 16)
POOL_GROUP = W_POOL // len(POOL_WINDOWS)
POOL_BUF = max(POOL_WINDOWS) - 1
N_EXPERTS = 32
TOP_K = 4
D_FF = D_MODEL
SWIGLU_LIMIT = 7.0
SWIGLU_ALPHA = 1.702
MOE_BLOCK = 128
IN_CONV = 2 * W_CONV
IN_SSM = 2 * W_SSM + 2 * SSM_GROUPS * SSM_STATE + SSM_HEADS
IN_RET = 2 * RET_HEADS * RET_KDIM + 2 * W_RET
IN_POOL = W_POOL
D_IN = IN_CONV + IN_SSM + IN_RET + IN_POOL

kernel_name = 'hybrid_streaming_encoder_step'

F32 = jnp.float32


def rmsnorm(x, g):
    xf = x.astype(F32)
    y = xf * lax.rsqrt(jnp.mean(xf * xf, axis=-1, keepdims=True) + EPS)
    return (y * g.astype(F32)).astype(x.dtype)


def layernorm(x, g, b):
    xf = x.astype(F32)
    xc = xf - jnp.mean(xf, axis=-1, keepdims=True)
    var = jnp.mean(xc * xc, axis=-1, keepdims=True)
    return (xc * lax.rsqrt(var + EPS) * g.astype(F32) + b.astype(F32)).astype(x.dtype)


def causal_dwconv(u, buf, w, b):
    k = w.shape[0]
    cat = jnp.concatenate([buf.astype(u.dtype), u], axis=1)
    y = lax.conv_general_dilated(cat, w[:, None, :].astype(u.dtype), (1,), 'VALID',
                                 dimension_numbers=('NWC', 'WIO', 'NWC'),
                                 feature_group_count=u.shape[-1])
    return y + b.astype(u.dtype), cat[:, cat.shape[1] - (k - 1):]


def conv_module(u_in, buf, conv_w, conv_b, ln_g, ln_b):
    val, gate = jnp.split(u_in, 2, axis=-1)
    u = val * jax.nn.sigmoid(gate)
    y, new_buf = causal_dwconv(u, buf, conv_w, conv_b)
    return jax.nn.silu(layernorm(y, ln_g, ln_b)), new_buf


def ssd_mixer(u_in, conv_buf, h0, conv_w, conv_b, dt_bias, a_log, d_skip, norm_g):
    bsz, L, _ = u_in.shape
    H, P, G, N = SSM_HEADS, SSM_HEADDIM, SSM_GROUPS, SSM_STATE
    z, xbc, dt_raw = jnp.split(u_in, [W_SSM, W_SSM + SSM_XBC], axis=-1)
    xbc, new_conv_buf = causal_dwconv(xbc, conv_buf, conv_w, conv_b)
    xbc = jax.nn.silu(xbc)
    xs, bm, cm = jnp.split(xbc, [W_SSM, W_SSM + G * N], axis=-1)
    lc = min(CHUNK, L)
    nc = L // lc
    dt = jax.nn.softplus(dt_raw.astype(F32) + dt_bias.astype(F32)).reshape(bsz, nc, lc, H)
    cum = jnp.cumsum(dt * (-jnp.exp(a_log.astype(F32))), axis=2)
    x = xs.astype(F32).reshape(bsz, nc, lc, H, P)
    xdt = x * dt[..., None]
    bh = jnp.repeat(bm.astype(F32).reshape(bsz, nc, lc, G, N), H // G, axis=3)
    ch = jnp.repeat(cm.astype(F32).reshape(bsz, nc, lc, G, N), H // G, axis=3)
    causal = jnp.tril(jnp.ones((lc, lc), dtype=bool))[None, None, :, :, None]
    seg = cum[:, :, :, None, :] - cum[:, :, None, :, :]
    decay = jnp.exp(jnp.where(causal, seg, -jnp.inf))
    scores = jnp.einsum('bclhn,bcshn->bclsh', ch, bh) * decay
    y = jnp.einsum('bclsh,bcshp->bclhp', scores, xdt)
    decay_to_end = jnp.exp(cum[:, :, -1:, :] - cum)
    chunk_states = jnp.einsum('bcshn,bcsh,bcshp->bchpn', bh, decay_to_end, xdt)
    chunk_decay = jnp.exp(cum[:, :, -1, :])

    def step(h, inp):
        st, dc = inp
        return h * dc[:, :, None, None] + st, h

    h_final, h_start = lax.scan(step, h0.astype(F32),
                                (jnp.moveaxis(chunk_states, 1, 0), jnp.moveaxis(chunk_decay, 1, 0)))
    h_start = jnp.moveaxis(h_start, 0, 1)
    y = y + jnp.einsum('bclhn,bchpn,bclh->bclhp', ch, h_start, jnp.exp(cum))
    y = y + x * d_skip.astype(F32)[:, None]
    y = y.reshape(bsz, L, W_SSM) * jax.nn.silu(z.astype(F32))
    return rmsnorm(y, norm_g).astype(u_in.dtype), new_conv_buf, h_final.astype(h0.dtype)


def rotary(x, pos):
    half = x.shape[-1] // 2
    inv = 1.0 / (ROPE_BASE ** jnp.linspace(0.0, 1.0, half, dtype=F32))
    ang = pos[:, None] * inv[None, :]
    cos = jnp.cos(ang)[None, :, None, :]
    sin = jnp.sin(ang)[None, :, None, :]
    x1, x2 = x[..., :half], x[..., half:]
    return jnp.concatenate([x1 * cos - x2 * sin, x1 * sin + x2 * cos], axis=-1)


def retention_mixer(u_in, s0, pos0):
    bsz, L, _ = u_in.shape
    H, dk, dv = RET_HEADS, RET_KDIM, RET_VDIM
    q, k, v, g = jnp.split(u_in, [H * dk, 2 * H * dk, 2 * H * dk + W_RET], axis=-1)
    pos = pos0 + jnp.arange(L, dtype=F32)
    q = rotary(q.astype(F32).reshape(bsz, L, H, dk), pos)
    k = rotary(k.astype(F32).reshape(bsz, L, H, dk), pos) * (dk ** -0.5)
    v = v.astype(F32).reshape(bsz, L, H, dv)
    lc = min(CHUNK, L)
    nc = L // lc
    qc = q.reshape(bsz, nc, lc, H, dk)
    kc = k.reshape(bsz, nc, lc, H, dk)
    vc = v.reshape(bsz, nc, lc, H, dv)
    log_gamma = jnp.log(1.0 - 2.0 ** (-5.0 - jnp.arange(H, dtype=F32)))
    idx = jnp.arange(lc, dtype=F32)
    intra = jnp.exp(jnp.abs(idx[:, None] - idx[None, :])[:, :, None] * log_gamma)
    scores = jnp.einsum('bclhd,bcshd->bclsh', qc, kc) * intra
    o = jnp.einsum('bclsh,bcshe->bclhe', scores, vc)
    k_dec = jnp.exp((lc - idx)[:, None] * log_gamma)
    chunk_kv = jnp.einsum('bcshd,sh,bcshe->bchde', kc, k_dec, vc)
    chunk_decay = jnp.exp(lc * log_gamma)

    def step(s, kv):
        return s * chunk_decay[None, :, None, None] + kv, s

    s_final, s_start = lax.scan(step, s0.astype(F32), jnp.moveaxis(chunk_kv, 1, 0))
    s_start = jnp.moveaxis(s_start, 0, 1)
    q_dec = jnp.exp(idx[:, None] * log_gamma)
    o = o + jnp.einsum('bclhd,lh,bchde->bclhe', qc, q_dec, s_start)
    o = o * lax.rsqrt(jnp.mean(o * o, axis=-1, keepdims=True) + EPS)
    o = o.reshape(bsz, L, W_RET) * jax.nn.silu(g.astype(F32))
    return o.astype(u_in.dtype), s_final.astype(s0.dtype)


def pool_mixer(p, buf, pos0, pool_w, pool_scale):
    bsz, L, _ = p.shape
    cat = jnp.concatenate([buf.astype(p.dtype), p], axis=1)
    cs = jnp.concatenate([jnp.zeros((bsz, 1, W_POOL), F32), jnp.cumsum(cat.astype(F32), axis=1)], axis=1)
    pos = pos0 + jnp.arange(L)
    pf = p.astype(F32)
    groups = []
    for gi, w in enumerate(POOL_WINDOWS):
        lo, hi = gi * POOL_GROUP, (gi + 1) * POOL_GROUP
        win_sum = cs[:, POOL_BUF + 1:, lo:hi] - cs[:, POOL_BUF + 1 - w:POOL_BUF + 1 - w + L, lo:hi]
        cnt = jnp.minimum(pos + 1, w).astype(F32)[None, :, None]
        groups.append(win_sum / cnt - pf[:, :, lo:hi])
    pooled = jnp.stack(groups, axis=2)
    y = jnp.einsum('blgc,gcd->blgd', pooled, pool_w.astype(F32)).reshape(bsz, L, W_POOL)
    return (y * pool_scale.astype(F32)).astype(p.dtype), cat[:, cat.shape[1] - POOL_BUF:]


def moe_ffn(h, router_w, router_b, w_gu, b_gu, w_down, b_down):
    T, D = h.shape
    logits = h.astype(F32) @ router_w.astype(F32) + router_b.astype(F32)
    top_logit, top_idx = lax.top_k(logits, TOP_K)
    gates = jax.nn.softmax(top_logit, axis=-1)
    M = T * TOP_K
    e_flat = top_idx.reshape(M)
    order = jnp.argsort(e_flat)
    e_sorted = e_flat[order]
    tok_sorted = order // TOP_K
    gate_sorted = gates.reshape(M)[order]
    counts = jnp.zeros((N_EXPERTS,), jnp.int32).at[e_flat].add(1)
    padded = (counts + MOE_BLOCK - 1) // MOE_BLOCK * MOE_BLOCK
    pad_end = jnp.cumsum(padded)
    pad_start = pad_end - padded
    start = jnp.cumsum(counts) - counts
    dest = pad_start[e_sorted] + jnp.arange(M) - start[e_sorted]
    n_blocks = (M + MOE_BLOCK - 1) // MOE_BLOCK + N_EXPERTS
    xbuf = jnp.zeros((n_blocks * MOE_BLOCK, D), h.dtype).at[dest].set(h[tok_sorted])
    block_expert = jnp.minimum(jnp.searchsorted(pad_end, jnp.arange(n_blocks) * MOE_BLOCK, side='right'),
                               N_EXPERTS - 1)

    def expert_block(args):
        xb, e = args
        gu = xb @ w_gu[e] + b_gu[e]
        g, u = jnp.split(gu, 2, axis=-1)
        g = jnp.minimum(g, SWIGLU_LIMIT)
        u = jnp.clip(u, -SWIGLU_LIMIT, SWIGLU_LIMIT)
        act = (u + 1.0) * (g * jax.nn.sigmoid(SWIGLU_ALPHA * g))
        return act @ w_down[e] + b_down[e]

    ybuf = lax.map(expert_block, (xbuf.reshape(n_blocks, MOE_BLOCK, D), block_expert)).reshape(-1, D)
    y_rows = ybuf[dest] * gate_sorted[:, None].astype(ybuf.dtype)
    return jnp.zeros((T, D), ybuf.dtype).at[tok_sorted].add(y_rows).astype(h.dtype)


def trunk_layer(x, c, st_conv, st_ssm_conv, st_ssm, st_ret, st_pool, pos0,
                w_ada, b_ada, norm1_g, norm2_g, w_in, w_out,
                conv_w, conv_b, conv_ln_g, conv_ln_b,
                ssm_conv_w, ssm_conv_b, ssm_dt_bias, ssm_a_log, ssm_d, ssm_norm_g,
                pool_w, pool_scale, router_w, router_b, moe_w_gu, moe_b_gu, moe_w_down, moe_b_down):
    bsz, L, D = x.shape
    mod = (jax.nn.silu(c.astype(F32)) @ w_ada.astype(F32) + b_ada.astype(F32)).astype(x.dtype)
    sh1, sc1, g1, sh2, sc2, g2 = jnp.split(mod, 6, axis=-1)
    h = rmsnorm(x, norm1_g) * (1.0 + sc1[:, None, :]) + sh1[:, None, :]
    u = h @ w_in
    u_conv, u_ssm, u_ret, u_pool = jnp.split(
        u, [IN_CONV, IN_CONV + IN_SSM, IN_CONV + IN_SSM + IN_RET], axis=-1)
    ya, n_conv = conv_module(u_conv, st_conv, conv_w, conv_b, conv_ln_g, conv_ln_b)
    yb, n_ssm_conv, n_ssm = ssd_mixer(u_ssm, st_ssm_conv, st_ssm, ssm_conv_w, ssm_conv_b,
                                      ssm_dt_bias, ssm_a_log, ssm_d, ssm_norm_g)
    yc, n_ret = retention_mixer(u_ret, st_ret, pos0)
    yd, n_pool = pool_mixer(u_pool, st_pool, pos0, pool_w, pool_scale)
    y = jnp.concatenate([ya, yb, yc, yd], axis=-1) @ w_out
    x = x + g1[:, None, :] * y
    h2 = rmsnorm(x, norm2_g) * (1.0 + sc2[:, None, :]) + sh2[:, None, :]
    f = moe_ffn(h2.reshape(bsz * L, D), router_w, router_b, moe_w_gu, moe_b_gu, moe_w_down, moe_b_down)
    x = x + g2[:, None, :] * f.reshape(bsz, L, D)
    return x, n_conv, n_ssm_conv, n_ssm, n_ret, n_pool


def setup_inputs(seed: int = 0) -> dict:
    key = jax.random.key(seed)
    ks = list(jax.random.split(key, 48))

    def nrm(shape, s):
        return jax.random.normal(ks.pop(), shape, F32) * s

    def gain(shape):
        return 1.0 + nrm(shape, 0.05)

    dt = jnp.exp(jax.random.uniform(ks.pop(), (DEPTH, SSM_HEADS), F32,
                                    minval=float(np.log(1e-3)), maxval=float(np.log(1e-1))))
    a_init = jax.random.uniform(ks.pop(), (DEPTH, SSM_HEADS), F32, minval=1.0, maxval=16.0)
    return {
        'x_prompt': nrm((BATCH, SEQ, D_MODEL), 1.0),
        'x_sample': nrm((DEC_BATCH, DEC_SEQ, D_MODEL), 1.0),
        'c_prompt': nrm((BATCH, D_MODEL), 1.0),
        'c_sample': nrm((DEC_BATCH, D_MODEL), 1.0),
        'state_conv': nrm((DEPTH, DEC_BATCH, CONV_K - 1, W_CONV), 0.5),
        'state_ssm_conv': nrm((DEPTH, DEC_BATCH, SSM_CONV_K - 1, SSM_XBC), 1.0),
        'state_ssm': nrm((DEPTH, DEC_BATCH, SSM_HEADS, SSM_HEADDIM, SSM_STATE), 0.1),
        'state_ret': nrm((DEPTH, DEC_BATCH, RET_HEADS, RET_KDIM, RET_VDIM), 1.0),
        'state_pool': nrm((DEPTH, DEC_BATCH, POOL_BUF, W_POOL), 1.0),
        'w_ada': nrm((DEPTH, D_MODEL, 6 * D_MODEL), 0.5 * D_MODEL ** -0.5),
        'b_ada': nrm((DEPTH, 6 * D_MODEL), 0.02),
        'norm1_g': gain((DEPTH, D_MODEL)),
        'norm2_g': gain((DEPTH, D_MODEL)),
        'w_in': nrm((DEPTH, D_MODEL, D_IN), D_MODEL ** -0.5),
        'w_out': nrm((DEPTH, D_MODEL, D_MODEL), D_MODEL ** -0.5),
        'conv_w': nrm((DEPTH, CONV_K, W_CONV), CONV_K ** -0.5),
        'conv_b': nrm((DEPTH, W_CONV), 0.02),
        'conv_ln_g': gain((DEPTH, W_CONV)),
        'conv_ln_b': nrm((DEPTH, W_CONV), 0.02),
        'ssm_conv_w': nrm((DEPTH, SSM_CONV_K, SSM_XBC), SSM_CONV_K ** -0.5),
        'ssm_conv_b': nrm((DEPTH, SSM_XBC), 0.02),
        'ssm_dt_bias': dt + jnp.log(-jnp.expm1(-dt)),
        'ssm_a_log': jnp.log(a_init),
        'ssm_d': 1.0 + nrm((DEPTH, SSM_HEADS), 0.1),
        'ssm_norm_g': gain((DEPTH, W_SSM)),
        'pool_w': nrm((DEPTH, len(POOL_WINDOWS), POOL_GROUP, POOL_GROUP), POOL_GROUP ** -0.5),
        'pool_scale': 1.0 + nrm((DEPTH, W_POOL), 0.1),
        'router_w': nrm((DEPTH, D_MODEL, N_EXPERTS), D_MODEL ** -0.5),
        'router_b': nrm((DEPTH, N_EXPERTS), 0.01),
        'moe_w_gu': nrm((DEPTH, N_EXPERTS, D_MODEL, 2 * D_FF), D_MODEL ** -0.5),
        'moe_b_gu': nrm((DEPTH, N_EXPERTS, 2 * D_FF), 0.01),
        'moe_w_down': nrm((DEPTH, N_EXPERTS, D_FF, D_MODEL), D_FF ** -0.5),
        'moe_b_down': nrm((DEPTH, N_EXPERTS, D_MODEL), 0.01),
        'final_norm_g': gain((D_MODEL,)),
    }


def reference(x_prompt, x_sample, c_prompt, c_sample,
              state_conv, state_ssm_conv, state_ssm, state_ret, state_pool,
              w_ada, b_ada, norm1_g, norm2_g, w_in, w_out,
              conv_w, conv_b, conv_ln_g, conv_ln_b,
              ssm_conv_w, ssm_conv_b, ssm_dt_bias, ssm_a_log, ssm_d, ssm_norm_g,
              pool_w, pool_scale, router_w, router_b, moe_w_gu, moe_b_gu, moe_w_down, moe_b_down,
              final_norm_g):
    layer_w = (w_ada, b_ada, norm1_g, norm2_g, w_in, w_out,
               conv_w, conv_b, conv_ln_g, conv_ln_b,
               ssm_conv_w, ssm_conv_b, ssm_dt_bias, ssm_a_log, ssm_d, ssm_norm_g,
               pool_w, pool_scale, router_w, router_b, moe_w_gu, moe_b_gu, moe_w_down, moe_b_down)
    bp = x_prompt.shape[0]
    dtp = x_prompt.dtype
    zero_states = (jnp.zeros((bp, CONV_K - 1, W_CONV), dtp),
                   jnp.zeros((bp, SSM_CONV_K - 1, SSM_XBC), dtp),
                   jnp.zeros((bp, SSM_HEADS, SSM_HEADDIM, SSM_STATE), dtp),
                   jnp.zeros((bp, RET_HEADS, RET_KDIM, RET_VDIM), dtp),
                   jnp.zeros((bp, POOL_BUF, W_POOL), dtp))
    xp, xs = x_prompt, x_sample
    new_p = ([], [], [], [], [])
    new_s = ([], [], [], [], [])
    for l in range(DEPTH):
        wl = [w[l] for w in layer_w]
        xp, *sp = trunk_layer(xp, c_prompt, *zero_states, 0, *wl)
        xs, *ss = trunk_layer(xs, c_sample, state_conv[l], state_ssm_conv[l], state_ssm[l],
                              state_ret[l], state_pool[l], PAST_LEN, *wl)
        for lst, val in zip(new_p, sp):
            lst.append(val)
        for lst, val in zip(new_s, ss):
            lst.append(val)
    y_prompt = rmsnorm(xp, final_norm_g)
    y_sample = rmsnorm(xs, final_norm_g)
    conv_p, ssm_conv_p, ssm_p, ret_p, pool_p = [jnp.stack(v, axis=0) for v in new_p]
    conv_s, ssm_conv_s, ssm_s, ret_s, pool_s = [jnp.stack(v, axis=0) for v in new_s]
    return (y_prompt, y_sample, conv_p, ssm_conv_p, ssm_p, ret_p, pool_p,
            conv_s, ssm_conv_s, ssm_s, ret_s, pool_s)
```

```python
import functools
import math

import numpy as np
import jax
import jax.numpy as jnp
from jax import lax
from jax.experimental import pallas as pl
from jax.experimental.pallas import tpu as pltpu

F32 = jnp.float32
BF16 = jnp.bfloat16
HI = lax.Precision.HIGHEST

EPS = 1e-6
CHUNK = 64
PAST_LEN = 1024
CONV_K = 31
SSM_CONV_K = 4
SSM_HEADS = 8
SSM_HEADDIM = 64
SSM_GROUPS = 2
SSM_STATE = 128
RET_HEADS = 4
RET_KDIM = 64
RET_VDIM = 128
ROPE_BASE = 10000.0
POOL_WINDOWS = (2, 4, 8, 16)
POOL_BUF = 15
TOP_K = 4
SWIGLU_LIMIT = 7.0
SWIGLU_ALPHA = 1.702

W_MIX = 512
LANES = 128
DT_PAD = LANES
VMEM_LIMIT = 56 * 1024 * 1024

MOE_SUPER = 1024
MOE_SUB = 256
MOE_TF = 256


def _cparams(sem, vmem=VMEM_LIMIT):
    return pltpu.CompilerParams(dimension_semantics=sem, vmem_limit_bytes=vmem)


def _silu(x):
    return x * jax.nn.sigmoid(x)


def _ada_body(c_ref, w_ref, b_ref, o_ref):
    c = c_ref[...]
    o_ref[...] = jnp.dot(_silu(c), w_ref[...], preferred_element_type=F32, precision=HI) + b_ref[...]


def _ada(c_all, w_ada, b_ada):
    nl, d, n6 = w_ada.shape
    cb = c_all.shape[0]
    tn = 1024
    return pl.pallas_call(
        _ada_body,
        out_shape=jax.ShapeDtypeStruct((nl, cb, n6), F32),
        grid=(nl, n6 // tn),
        in_specs=[pl.BlockSpec((cb, d), lambda l, j: (0, 0)),
                  pl.BlockSpec((None, d, tn), lambda l, j: (l, 0, j)),
                  pl.BlockSpec((None, 1, tn), lambda l, j: (l, 0, j))],
        out_specs=pl.BlockSpec((None, cb, tn), lambda l, j: (l, 0, j)),
        compiler_params=_cparams(("arbitrary", "arbitrary")),
        name="ada",
    )(c_all, w_ada, b_ada.reshape(nl, 1, n6))


def _inproj_body(x_ref, sh_ref, sc_ref, g_ref, wc_ref, ws_ref, wr_ref, wp_ref, uc_ref, us_ref, ur_ref, up_ref):
    nb, r, d = x_ref.shape
    x = x_ref[...]
    ms = jnp.mean(x * x, axis=-1, keepdims=True)
    h = x * lax.rsqrt(ms + EPS) * g_ref[...]
    h = h * (1.0 + sc_ref[...]) + sh_ref[...]
    hb = h.reshape(nb * r, d).astype(BF16)
    for w_ref, u_ref in ((wc_ref, uc_ref), (ws_ref, us_ref), (wr_ref, ur_ref), (wp_ref, up_ref)):
        u = jnp.dot(hb, w_ref[...], preferred_element_type=F32)
        u_ref[...] = u.reshape(nb, r, u.shape[-1])


def _inproj(x, mod, norm_g, ws, layer, nb, r):
    b, l, d = x.shape
    grid = (b // nb, l // r)
    xs = pl.BlockSpec((nb, r, d), lambda i, j: (i, j, 0))
    wspecs = [pl.BlockSpec((None, d, w.shape[-1]), lambda i, j: (layer, 0, 0), pipeline_mode=pl.Buffered(1)) for w in ws]
    return pl.pallas_call(
        _inproj_body,
        out_shape=[jax.ShapeDtypeStruct((b, l, w.shape[-1]), F32) for w in ws],
        grid=grid,
        in_specs=[xs,
                  pl.BlockSpec((nb, 1, d), lambda i, j: (i, 0, 0)),
                  pl.BlockSpec((nb, 1, d), lambda i, j: (i, 0, 1)),
                  pl.BlockSpec((None, 1, d), lambda i, j: (layer, 0, 0))] + wspecs,
        out_specs=[pl.BlockSpec((nb, r, w.shape[-1]), lambda i, j: (i, j, 0)) for w in ws],
        compiler_params=_cparams(("arbitrary", "arbitrary")),
        name="inproj",
    )(x, mod, mod, norm_g, *ws)


CONV_HIST = 32
CONV_SUB = 64


def _conv_body(u_ref, st_ref, w_ref, b_ref, lg_ref, lb_ref, y_ref, ns_ref, hist_ref):
    r = u_ref.shape[0]
    wm = W_MIX

    @pl.when(pl.program_id(1) == 0)
    def _():
        hist_ref[0:CONV_HIST, :] = st_ref[...]

    u = u_ref[...]
    hist_ref[CONV_HIST:CONV_HIST + r, :] = u[:, :wm] * jax.nn.sigmoid(u[:, wm:])
    sub = min(CONV_SUB, r)
    for r0 in range(0, r, sub):
        acc = jnp.zeros((sub, wm), F32)
        for j in range(CONV_K):
            lo = CONV_HIST + r0 - j
            acc = acc + w_ref[CONV_K - 1 - j:CONV_K - j, :] * hist_ref[lo:lo + sub, :]
        acc = acc + b_ref[...]
        mu = jnp.mean(acc, axis=-1, keepdims=True)
        xc = acc - mu
        var = jnp.mean(xc * xc, axis=-1, keepdims=True)
        yn = xc * lax.rsqrt(var + EPS) * lg_ref[...] + lb_ref[...]
        y_ref[r0:r0 + sub, :] = _silu(yn).astype(y_ref.dtype)
    last = hist_ref[r:r + CONV_HIST, :]
    ns_ref[...] = last
    hist_ref[0:CONV_HIST, :] = last


def _conv_mixer(u, st, conv_w, conv_b, ln_g, ln_b, layer, r):
    b, l, _ = u.shape
    wm = W_MIX
    par = lambda k: pl.BlockSpec((None, k, wm), lambda i, j: (layer, 0, 0))
    return pl.pallas_call(
        _conv_body,
        out_shape=[jax.ShapeDtypeStruct((b, l, wm), BF16), jax.ShapeDtypeStruct((b, CONV_HIST, wm), F32)],
        grid=(b, l // r),
        in_specs=[pl.BlockSpec((None, r, 2 * wm), lambda i, j: (i, j, 0)),
                  pl.BlockSpec((None, CONV_HIST, wm), lambda i, j: (i, 0, 0)),
                  par(CONV_K), par(1), par(1), par(1)],
        out_specs=[pl.BlockSpec((None, r, wm), lambda i, j: (i, j, 0)),
                   pl.BlockSpec((None, CONV_HIST, wm), lambda i, j: (i, 0, 0))],
        scratch_shapes=[pltpu.VMEM((r + CONV_HIST, wm), F32)],
        compiler_params=_cparams(("arbitrary", "arbitrary")),
        name="conv_mixer",
    )(u, st, conv_w, conv_b, ln_g, ln_b)


POOL_HIST = 16


def _pool_body(p_ref, st_ref, w_ref, sc_ref, y_ref, hist_ref, *, pos0):
    r = p_ref.shape[0]
    i = pl.program_id(1)

    @pl.when(i == 0)
    def _():
        hist_ref[0:POOL_HIST, :] = st_ref[...]

    p = p_ref[...]
    hist_ref[POOL_HIST:POOL_HIST + r, :] = p
    pos = pos0 + i * r + lax.broadcasted_iota(jnp.int32, (r, 1), 0)
    outs = []
    for gi, w in enumerate(POOL_WINDOWS):
        lo = gi * LANES
        s = p[:, lo:lo + LANES]
        for j in range(1, w):
            s = s + hist_ref[POOL_HIST - j:POOL_HIST - j + r, lo:lo + LANES]
        cnt = jnp.minimum(pos + 1, w).astype(F32)
        pooled = s / cnt - p[:, lo:lo + LANES]
        outs.append(jnp.dot(pooled.astype(BF16), w_ref[gi], preferred_element_type=F32))
    y = jnp.concatenate(outs, axis=-1) * sc_ref[...]
    y_ref[...] = y.astype(y_ref.dtype)
    hist_ref[0:POOL_HIST, :] = hist_ref[r:r + POOL_HIST, :]


def _pool_mixer(p, st, pool_w, pool_scale, layer, r, pos0):
    b, l, wm = p.shape
    return pl.pallas_call(
        functools.partial(_pool_body, pos0=pos0),
        out_shape=jax.ShapeDtypeStruct((b, l, wm), BF16),
        grid=(b, l // r),
        in_specs=[pl.BlockSpec((None, r, wm), lambda i, j: (i, j, 0)),
                  pl.BlockSpec((None, POOL_HIST, wm), lambda i, j: (i, 0, 0)),
                  pl.BlockSpec((None, len(POOL_WINDOWS), LANES, LANES), lambda i, j: (layer, 0, 0, 0)),
                  pl.BlockSpec((None, 1, wm), lambda i, j: (layer, 0, 0))],
        out_specs=pl.BlockSpec((None, r, wm), lambda i, j: (i, j, 0)),
        scratch_shapes=[pltpu.VMEM((r + POOL_HIST, wm), F32)],
        compiler_params=_cparams(("arbitrary", "arbitrary")),
        name="pool_mixer",
    )(p, st, pool_w, pool_scale)


SSD_HIST = 8
SSM_XBC = W_MIX + 2 * SSM_GROUPS * SSM_STATE
GROUP_W = W_MIX // SSM_GROUPS


def _ssd_body(u_ref, cst_ref, hst_ref, cw_ref, cb_ref, dtb_ref, alog_ref, dsk_ref, ng_ref, ex_ref,
              y_ref, nh_ref, hist_ref, st_ref, *, q):
    r = u_ref.shape[0]
    wm = W_MIX
    n = SSM_STATE
    i = pl.program_id(1)

    @pl.when(i == 0)
    def _():
        hist_ref[0:SSD_HIST, :] = cst_ref[...]
        st_ref[...] = hst_ref[...].T

    hist_ref[SSD_HIST:SSD_HIST + r, :] = u_ref[:, wm:wm + SSM_XBC]
    a_neg = -jnp.exp(alog_ref[...])
    ex = ex_ref[...]
    row = lax.broadcasted_iota(jnp.int32, (q, q), 0)
    col = lax.broadcasted_iota(jnp.int32, (q, q), 1)
    tril = row >= col
    ltri = tril.astype(F32)
    eye = (lax.broadcasted_iota(jnp.int32, (LANES, LANES), 0)
           == lax.broadcasted_iota(jnp.int32, (LANES, LANES), 1)).astype(F32)
    head_of_lane = lax.broadcasted_iota(jnp.int32, (q, GROUP_W), 1) // SSM_HEADDIM
    heads_per_group = SSM_HEADS // SSM_GROUPS

    for c in range(r // q):
        r0 = c * q
        xc = jnp.zeros((q, SSM_XBC), F32)
        for k in range(SSM_CONV_K):
            lo = SSD_HIST + r0 + k - (SSM_CONV_K - 1)
            xc = xc + cw_ref[k:k + 1, :] * hist_ref[lo:lo + q, :]
        xc = _silu(xc + cb_ref[...])
        xs = xc[:, :wm]
        bm = xc[:, wm:wm + SSM_GROUPS * n]
        cm = xc[:, wm + SSM_GROUPS * n:]
        z = u_ref[r0:r0 + q, 0:wm]
        dt = jax.nn.softplus(u_ref[r0:r0 + q, wm + SSM_XBC:wm + SSM_XBC + DT_PAD] + dtb_ref[...])
        cum = jnp.dot(ltri, dt * a_neg, preferred_element_type=F32, precision=HI)
        cum_t = lax.dot_general(eye, cum, (((1,), (1,)), ((), ())), preferred_element_type=F32, precision=HI)
        dt_full = jnp.dot(dt, ex, preferred_element_type=F32, precision=HI)
        cum_full = jnp.dot(cum, ex, preferred_element_type=F32, precision=HI)
        cum_last = cum_full[q - 1:q, :]
        xdt = xs * dt_full
        xw = xdt * jnp.exp(cum_last - cum_full)
        ecum = jnp.exp(cum_full)
        ys = []
        for g in range(SSM_GROUPS):
            gl = slice(g * GROUP_W, (g + 1) * GROUP_W)
            bg = bm[:, g * n:(g + 1) * n].astype(BF16)
            cg = cm[:, g * n:(g + 1) * n].astype(BF16)
            cb = lax.dot_general(cg, bg, (((1,), (1,)), ((), ())), preferred_element_type=F32)
            xdt_g = xdt[:, gl].astype(BF16)
            yg = jnp.zeros((q, GROUP_W), F32)
            for hh in range(heads_per_group):
                h = g * heads_per_group + hh
                seg = cum_full[:, h * SSM_HEADDIM:h * SSM_HEADDIM + 1] - cum_t[h:h + 1, :]
                dec = jnp.exp(jnp.where(tril, seg, -jnp.inf))
                ph = jnp.dot((cb * dec).astype(BF16), xdt_g, preferred_element_type=F32)
                yg = jnp.where(head_of_lane == hh, ph, yg)
            st_g = st_ref[:, gl]
            yg = yg + ecum[:, gl] * jnp.dot(cg, st_g.astype(BF16), preferred_element_type=F32)
            upd = lax.dot_general(bg, xw[:, gl].astype(BF16), (((0,), (0,)), ((), ())), preferred_element_type=F32)
            st_ref[:, gl] = st_g * jnp.exp(cum_last[:, gl]) + upd
            ys.append(yg)
        y = jnp.concatenate(ys, axis=-1) + xs * dsk_ref[...]
        y = y * _silu(z)
        y = y * lax.rsqrt(jnp.mean(y * y, axis=-1, keepdims=True) + EPS) * ng_ref[...]
        y_ref[r0:r0 + q, :] = y.astype(y_ref.dtype)

    hist_ref[0:SSD_HIST, :] = hist_ref[r:r + SSD_HIST, :]

    @pl.when(i == pl.num_programs(1) - 1)
    def _():
        nh_ref[...] = st_ref[...].T


def _ssd_mixer(u, cst, hst, cw, cb, dtb, alog, dsk, ng, ex, layer, r, q):
    b, l, wu = u.shape
    wm = W_MIX
    hp = SSM_HEADS * SSM_HEADDIM
    par = lambda k, w: pl.BlockSpec((None, k, w), lambda i, j: (layer, 0, 0))
    return pl.pallas_call(
        functools.partial(_ssd_body, q=q),
        out_shape=[jax.ShapeDtypeStruct((b, l, wm), BF16), jax.ShapeDtypeStruct((b, hp, SSM_STATE), F32)],
        grid=(b, l // r),
        in_specs=[pl.BlockSpec((None, r, wu), lambda i, j: (i, j, 0)),
                  pl.BlockSpec((None, SSD_HIST, SSM_XBC), lambda i, j: (i, 0, 0)),
                  pl.BlockSpec((None, hp, SSM_STATE), lambda i, j: (i, 0, 0)),
                  par(SSM_CONV_K, SSM_XBC), par(1, SSM_XBC), par(1, DT_PAD), par(1, DT_PAD), par(1, wm), par(1, wm),
                  pl.BlockSpec((DT_PAD, wm), lambda i, j: (0, 0))],
        out_specs=[pl.BlockSpec((None, r, wm), lambda i, j: (i, j, 0)),
                   pl.BlockSpec((None, hp, SSM_STATE), lambda i, j: (i, 0, 0))],
        scratch_shapes=[pltpu.VMEM((r + SSD_HIST, SSM_XBC), F32), pltpu.VMEM((SSM_STATE, hp), F32)],
        compiler_params=_cparams(("arbitrary", "arbitrary")),
        name="ssd_mixer",
    )(u, cst, hst, cw, cb, dtb, alog, dsk, ng, ex)


RET_QK = RET_HEADS * RET_KDIM
RET_HALF = RET_KDIM // 2


def _ret_body(u_ref, st_ref, cos_ref, sin_ref, intra_ref, kdec_ref, qdec_ref, cdec_ref, y_ref, ns_ref, s_ref, *, q):
    r = u_ref.shape[0]
    wm = W_MIX
    i = pl.program_id(1)
    dv = RET_VDIM

    @pl.when(i == 0)
    def _():
        s_ref[...] = jnp.zeros_like(s_ref)
        for h in range(RET_HEADS):
            for b in range(2):
                s_ref[LANES * b + RET_HALF * h:LANES * b + RET_HALF * (h + 1), dv * h:dv * (h + 1)] = (
                    st_ref[RET_KDIM * h + RET_HALF * b:RET_KDIM * h + RET_HALF * (b + 1), :])

    head_of_qlane = (lax.broadcasted_iota(jnp.int32, (q, RET_QK), 1) % LANES) // RET_HALF
    bd_mask = ((lax.broadcasted_iota(jnp.int32, (RET_QK, wm), 0) % LANES) // RET_HALF
               == lax.broadcasted_iota(jnp.int32, (RET_QK, wm), 1) // dv)

    for c in range(r // q):
        r0 = c * q
        cos = cos_ref[r0:r0 + q, :]
        sin = sin_ref[r0:r0 + q, :]
        q1 = u_ref[r0:r0 + q, 0:LANES]
        q2 = u_ref[r0:r0 + q, LANES:2 * LANES]
        k1 = u_ref[r0:r0 + q, RET_QK:RET_QK + LANES]
        k2 = u_ref[r0:r0 + q, RET_QK + LANES:2 * RET_QK]
        v = u_ref[r0:r0 + q, 2 * RET_QK:2 * RET_QK + wm]
        gate = u_ref[r0:r0 + q, 2 * RET_QK + wm:2 * RET_QK + 2 * wm]
        qr = jnp.concatenate([q1 * cos - q2 * sin, q1 * sin + q2 * cos], axis=-1)
        kr = jnp.concatenate([k1 * cos - k2 * sin, k1 * sin + k2 * cos], axis=-1) * (RET_KDIM ** -0.5)
        qrb = qr.astype(BF16)
        krb = kr.astype(BF16)
        vb = v.astype(BF16)
        outs = []
        for h in range(RET_HEADS):
            qm = jnp.where(head_of_qlane == h, qr, 0.0).astype(BF16)
            sc = lax.dot_general(qm, krb, (((1,), (1,)), ((), ())), preferred_element_type=F32)
            ph = (sc * intra_ref[h]).astype(BF16)
            outs.append(jnp.dot(ph, vb[:, dv * h:dv * (h + 1)], preferred_element_type=F32))
        o = jnp.concatenate(outs, axis=-1)
        s_old = s_ref[...]
        o = o + qdec_ref[...] * jnp.dot(qrb, s_old.astype(BF16), preferred_element_type=F32)
        kv = lax.dot_general(krb, (v * kdec_ref[...]).astype(BF16), (((0,), (0,)), ((), ())), preferred_element_type=F32)
        s_ref[...] = s_old * cdec_ref[...] + jnp.where(bd_mask, kv, 0.0)
        normed = []
        for h in range(RET_HEADS):
            oh = o[:, dv * h:dv * (h + 1)]
            normed.append(oh * lax.rsqrt(jnp.mean(oh * oh, axis=-1, keepdims=True) + EPS))
        o = jnp.concatenate(normed, axis=-1) * _silu(gate)
        y_ref[r0:r0 + q, :] = o.astype(y_ref.dtype)

    @pl.when(i == pl.num_programs(1) - 1)
    def _():
        for h in range(RET_HEADS):
            for b in range(2):
                ns_ref[RET_KDIM * h + RET_HALF * b:RET_KDIM * h + RET_HALF * (b + 1), :] = (
                    s_ref[LANES * b + RET_HALF * h:LANES * b + RET_HALF * (h + 1), dv * h:dv * (h + 1)])


def _ret_tables(l, q, pos0):
    inv = 1.0 / (ROPE_BASE ** jnp.linspace(0.0, 1.0, RET_HALF, dtype=F32))
    pos = pos0 + jnp.arange(l, dtype=F32)
    ang = pos[:, None] * inv[None, :]
    cos = jnp.tile(jnp.cos(ang), (1, RET_HEADS))
    sin = jnp.tile(jnp.sin(ang), (1, RET_HEADS))
    log_gamma = jnp.log(1.0 - 2.0 ** (-5.0 - jnp.arange(RET_HEADS, dtype=F32)))
    idx = jnp.arange(q, dtype=F32)
    intra = jnp.exp(jnp.abs(idx[:, None] - idx[None, :])[None, :, :] * log_gamma[:, None, None])
    kdec = jnp.repeat(jnp.exp((q - idx)[:, None] * log_gamma), RET_VDIM, axis=1)
    qdec = jnp.repeat(jnp.exp(idx[:, None] * log_gamma), RET_VDIM, axis=1)
    cdec = jnp.repeat(jnp.exp(q * log_gamma), RET_VDIM)[None, :]
    return cos, sin, intra, kdec, qdec, cdec


def _ret_mixer(u, st, tables, r, q):
    b, l, wu = u.shape
    wm = W_MIX
    cos, sin, intra, kdec, qdec, cdec = tables
    hk = RET_HEADS * RET_KDIM
    full = lambda a: pl.BlockSpec(a.shape, lambda i, j: (0,) * a.ndim)
    return pl.pallas_call(
        functools.partial(_ret_body, q=q),
        out_shape=[jax.ShapeDtypeStruct((b, l, wm), BF16), jax.ShapeDtypeStruct((b, hk, RET_VDIM), F32)],
        grid=(b, l // r),
        in_specs=[pl.BlockSpec((None, r, wu), lambda i, j: (i, j, 0)),
                  pl.BlockSpec((None, hk, RET_VDIM), lambda i, j: (i, 0, 0)),
                  pl.BlockSpec((r, LANES), lambda i, j: (j, 0)),
                  pl.BlockSpec((r, LANES), lambda i, j: (j, 0)),
                  full(intra), full(kdec), full(qdec), full(cdec)],
        out_specs=[pl.BlockSpec((None, r, wm), lambda i, j: (i, j, 0)),
                   pl.BlockSpec((None, hk, RET_VDIM), lambda i, j: (i, 0, 0))],
        scratch_shapes=[pltpu.VMEM((RET_QK, wm), F32)],
        compiler_params=_cparams(("arbitrary", "arbitrary")),
        name="ret_mixer",
    )(u, st, cos, sin, intra, kdec, qdec, cdec)


def _outproj_body(x_ref, ya_ref, yb_ref, yc_ref, yd_ref, g1_ref, sh_ref, sc_ref, ng_ref, wo_ref, rw_ref, rb_ref,
                  x1_ref, h2_ref, idx_ref, gate_ref):
    nb, r, d = x_ref.shape
    n = nb * r
    ycat = jnp.concatenate([ya_ref[...], yb_ref[...], yc_ref[...], yd_ref[...]], axis=-1).reshape(n, d)
    y = jnp.dot(ycat, wo_ref[...], preferred_element_type=F32).reshape(nb, r, d)
    x1 = x_ref[...] + g1_ref[...] * y
    x1_ref[...] = x1
    ms = jnp.mean(x1 * x1, axis=-1, keepdims=True)
    h2 = x1 * lax.rsqrt(ms + EPS) * ng_ref[...]
    h2 = (h2 * (1.0 + sc_ref[...]) + sh_ref[...]).reshape(n, d)
    h2_ref[...] = h2.astype(h2_ref.dtype).reshape(nb, r, d)
    logits = jnp.dot(h2, rw_ref[...], preferred_element_type=F32, precision=HI) + rb_ref[...]
    ne = logits.shape[-1]
    lane = lax.broadcasted_iota(jnp.int32, (n, ne), 1)
    vals, idxs = [], []
    for _ in range(TOP_K):
        m = jnp.max(logits, axis=-1, keepdims=True)
        idx = jnp.min(jnp.where(logits == m, lane, ne), axis=-1, keepdims=True)
        vals.append(m)
        idxs.append(idx)
        logits = jnp.where(lane == idx, -jnp.inf, logits)
    v = jnp.concatenate(vals, axis=-1)
    ex = jnp.exp(v - v[:, 0:1])
    gate_ref[...] = (ex / jnp.sum(ex, axis=-1, keepdims=True)).reshape(nb, r, TOP_K)
    idx_ref[...] = jnp.concatenate(idxs, axis=-1).reshape(nb, r, TOP_K)


def _outproj(x, ys, mod, norm_g, w_out, router_w, router_b, layer, nb, r):
    b, l, d = x.shape
    ne = router_w.shape[-1]
    blk = lambda w: pl.BlockSpec((nb, r, w), lambda i, j: (i, j, 0))
    modspec = lambda k: pl.BlockSpec((nb, 1, d), lambda i, j: (i, 0, k))
    return pl.pallas_call(
        _outproj_body,
        out_shape=[jax.ShapeDtypeStruct((b, l, d), F32), jax.ShapeDtypeStruct((b, l, d), BF16),
                   jax.ShapeDtypeStruct((b, l, TOP_K), jnp.int32), jax.ShapeDtypeStruct((b, l, TOP_K), F32)],
        grid=(b // nb, l // r),
        in_specs=[blk(d), blk(W_MIX), blk(W_MIX), blk(W_MIX), blk(W_MIX),
                  modspec(2), modspec(3), modspec(4),
                  pl.BlockSpec((None, 1, d), lambda i, j: (layer, 0, 0)),
                  pl.BlockSpec((None, d, d), lambda i, j: (layer, 0, 0), pipeline_mode=pl.Buffered(1)),
                  pl.BlockSpec((None, d, ne), lambda i, j: (layer, 0, 0)),
                  pl.BlockSpec((None, 1, ne), lambda i, j: (layer, 0, 0))],
        out_specs=[blk(d), blk(d), blk(TOP_K), blk(TOP_K)],
        compiler_params=_cparams(("arbitrary", "arbitrary")),
        name="outproj",
    )(x, *ys, mod, mod, mod, norm_g, w_out, router_w, router_b)


def _gather_body(valid_ref, tok_ref, h_ref, o_ref, sem):
    s = pl.program_id(0)

    @pl.when(valid_ref[s] > 0)
    def _():
        base = s * MOE_SUB

        def issue(r, c):
            pltpu.make_async_copy(h_ref.at[tok_ref[0, r]], o_ref.at[base + r], sem).start()
            return c

        lax.fori_loop(0, MOE_SUB, issue, 0)

        def drain(r, c):
            pltpu.make_async_copy(h_ref.at[0], o_ref.at[base + r], sem).wait()
            return c

        lax.fori_loop(0, MOE_SUB, drain, 0)


def _moe_gather(h2, row_token, sub_valid, n_rows):
    t, d = h2.shape
    n_sub = n_rows // MOE_SUB
    h3 = h2.reshape(t, d // LANES, LANES)
    out = pl.pallas_call(
        _gather_body,
        out_shape=jax.ShapeDtypeStruct((n_rows, d // LANES, LANES), h2.dtype),
        grid_spec=pltpu.PrefetchScalarGridSpec(
            num_scalar_prefetch=1,
            grid=(n_sub,),
            in_specs=[pl.BlockSpec((None, 1, MOE_SUB), lambda s, v: (s, 0, 0), memory_space=pltpu.SMEM),
                      pl.BlockSpec(memory_space=pl.ANY)],
            out_specs=pl.BlockSpec(memory_space=pl.ANY),
            scratch_shapes=[pltpu.SemaphoreType.DMA(())]),
        compiler_params=_cparams(("arbitrary",)),
        name="moe_gather",
    )(sub_valid, row_token.reshape(n_sub, 1, MOE_SUB), h3)
    return out.reshape(n_rows, d)


def _expert_body(e_ref, nv_ref, blk_ref, x_ref, wg_ref, wu_ref, bg_ref, bu_ref, wd_ref, bd_ref, o_ref,
                 wgb_ref, wub_ref, wdb_ref):
    b = pl.program_id(0)
    j = pl.program_id(1)
    nv = nv_ref[b]

    @pl.when(nv > 0)
    def _():
        wgb_ref[...] = wg_ref[...].astype(BF16)
        wub_ref[...] = wu_ref[...].astype(BF16)
        wdb_ref[...] = wd_ref[...].astype(BF16)
        for s in range(MOE_SUPER // MOE_SUB):
            rows = slice(s * MOE_SUB, (s + 1) * MOE_SUB)

            @pl.when(s * MOE_SUB < nv)
            def _():
                x = x_ref[rows, :]
                g = jnp.dot(x, wgb_ref[...], preferred_element_type=F32) + bg_ref[...]
                u = jnp.dot(x, wub_ref[...], preferred_element_type=F32) + bu_ref[...]
                g = jnp.minimum(g, SWIGLU_LIMIT)
                u = jnp.clip(u, -SWIGLU_LIMIT, SWIGLU_LIMIT)
                act = (u + 1.0) * (g * jax.nn.sigmoid(SWIGLU_ALPHA * g))
                y = jnp.dot(act.astype(BF16), wdb_ref[...], preferred_element_type=F32)

                @pl.when(j == 0)
                def _():
                    o_ref[rows, :] = y + bd_ref[...]

                @pl.when(j > 0)
                def _():
                    o_ref[rows, :] += y

            @pl.when(jnp.logical_and(j == 0, s * MOE_SUB >= nv))
            def _():
                o_ref[rows, :] = jnp.zeros((MOE_SUB, o_ref.shape[-1]), o_ref.dtype)


def _moe_experts(xbuf, sb_e, sb_nv, sb_blk, w_gu, b_gu, w_down, b_down, layer):
    n_rows, d = xbuf.shape
    nl, ne, _, f2 = w_gu.shape
    f = f2 // 2
    tf = min(MOE_TF, f)
    nj = f // tf
    n_sb = n_rows // MOE_SUPER

    def jj(b, j, nv):
        return jnp.where(nv[b] > 0, j, nj - 1)

    return pl.pallas_call(
        _expert_body,
        out_shape=jax.ShapeDtypeStruct((n_rows, d), F32),
        grid_spec=pltpu.PrefetchScalarGridSpec(
            num_scalar_prefetch=3,
            grid=(n_sb, nj),
            in_specs=[
                pl.BlockSpec((MOE_SUPER, d), lambda b, j, e, nv, blk: (blk[b], 0)),
                pl.BlockSpec((None, None, d, tf), lambda b, j, e, nv, blk: (layer, e[b], 0, jj(b, j, nv))),
                pl.BlockSpec((None, None, d, tf), lambda b, j, e, nv, blk: (layer, e[b], 0, nj + jj(b, j, nv))),
                pl.BlockSpec((None, None, 1, tf), lambda b, j, e, nv, blk: (layer, e[b], 0, jj(b, j, nv))),
                pl.BlockSpec((None, None, 1, tf), lambda b, j, e, nv, blk: (layer, e[b], 0, nj + jj(b, j, nv))),
                pl.BlockSpec((None, None, tf, d), lambda b, j, e, nv, blk: (layer, e[b], jj(b, j, nv), 0)),
                pl.BlockSpec((None, None, 1, d), lambda b, j, e, nv, blk: (layer, e[b], 0, 0)),
            ],
            out_specs=pl.BlockSpec((MOE_SUPER, d), lambda b, j, e, nv, blk: (blk[b], 0)),
            scratch_shapes=[pltpu.VMEM((d, tf), BF16), pltpu.VMEM((d, tf), BF16), pltpu.VMEM((tf, d), BF16)]),
        compiler_params=_cparams(("arbitrary", "arbitrary")),
        name="moe_experts",
    )(sb_e, sb_nv, sb_blk, xbuf, w_gu, w_gu, b_gu.reshape(nl, ne, 1, f2), b_gu.reshape(nl, ne, 1, f2),
      w_down, b_down.reshape(nl, ne, 1, d))


def _combine_body(dest_ref, x_ref, g2_ref, gate_ref, y_ref, o_ref, gbuf_ref, sem):
    nb, r, d = x_ref.shape
    n = nb * r

    def issue(t, c):
        for k in range(TOP_K):
            pltpu.make_async_copy(y_ref.at[pl.ds(dest_ref[0, t * TOP_K + k], 1), :],
                                  gbuf_ref.at[k, pl.ds(t, 1), :], sem).start()
        return c

    lax.fori_loop(0, n, issue, 0)

    def drain(t, c):
        for k in range(TOP_K):
            pltpu.make_async_copy(y_ref.at[pl.ds(0, 1), :], gbuf_ref.at[k, pl.ds(t, 1), :], sem).wait()
        return c

    lax.fori_loop(0, n, drain, 0)
    gate = gate_ref[...].reshape(n, TOP_K)
    f = gbuf_ref[0] * gate[:, 0:1]
    for k in range(1, TOP_K):
        f = f + gbuf_ref[k] * gate[:, k:k + 1]
    o_ref[...] = x_ref[...] + g2_ref[...] * f.reshape(nb, r, d)


def _moe_combine(x1, mod, gates, dest, ybuf, nb, r):
    b, l, d = x1.shape
    n = nb * r
    nt = (b // nb) * (l // r)
    lt = l // r
    blk = lambda w: pl.BlockSpec((nb, r, w), lambda i, j: (i, j, 0))
    return pl.pallas_call(
        _combine_body,
        out_shape=jax.ShapeDtypeStruct((b, l, d), F32),
        grid=(b // nb, lt),
        in_specs=[pl.BlockSpec((None, 1, n * TOP_K), lambda i, j: (i * lt + j, 0, 0), memory_space=pltpu.SMEM),
                  blk(d),
                  pl.BlockSpec((nb, 1, d), lambda i, j: (i, 0, 5)),
                  blk(TOP_K),
                  pl.BlockSpec(memory_space=pl.ANY)],
        out_specs=blk(d),
        scratch_shapes=[pltpu.VMEM((TOP_K, n, d), F32), pltpu.SemaphoreType.DMA(())],
        compiler_params=_cparams(("arbitrary", "arbitrary")),
        name="moe_combine",
    )(dest.reshape(nt, 1, n * TOP_K), x1, mod, gates, ybuf)


def _route(idx_all, n_experts):
    m = idx_all.shape[0] * TOP_K
    e_flat = idx_all.reshape(m)
    onehot = (e_flat[:, None] == jnp.arange(n_experts, dtype=jnp.int32)[None, :]).astype(jnp.int32)
    csum = jnp.cumsum(onehot, axis=0)
    counts = csum[-1]
    rank = jnp.sum(csum * onehot, axis=1) - 1
    n_sb_e = (counts + MOE_SUPER - 1) // MOE_SUPER
    sb_end = jnp.cumsum(n_sb_e)
    sb_start = sb_end - n_sb_e
    dest = sb_start[e_flat] * MOE_SUPER + rank
    n_sb = -(-m // MOE_SUPER) + n_experts
    n_rows = n_sb * MOE_SUPER
    n_used = sb_end[-1]
    sb_id = jnp.arange(n_sb, dtype=jnp.int32)
    sb_clamped = jnp.minimum(sb_id, n_used - 1)
    sb_e = jnp.minimum(jnp.searchsorted(sb_end, sb_clamped, side='right'), n_experts - 1).astype(jnp.int32)
    within = sb_clamped - sb_start[sb_e]
    sb_nv = jnp.where(sb_id < n_used, jnp.clip(counts[sb_e] - within * MOE_SUPER, 0, MOE_SUPER), 0).astype(jnp.int32)
    row_token = jnp.zeros((n_rows,), jnp.int32).at[dest].set(jnp.arange(m, dtype=jnp.int32) // TOP_K)
    sub_per = MOE_SUPER // MOE_SUB
    sub_id = jnp.arange(n_sb * sub_per, dtype=jnp.int32)
    sub_valid = ((sub_id % sub_per) * MOE_SUB < sb_nv[sub_id // sub_per]).astype(jnp.int32)
    return dest.astype(jnp.int32), row_token, sub_valid, sb_e, sb_nv, sb_clamped.astype(jnp.int32), n_rows


def _final_body(x_ref, g_ref, o_ref):
    x = x_ref[...]
    o_ref[...] = x * lax.rsqrt(jnp.mean(x * x, axis=-1, keepdims=True) + EPS) * g_ref[...]


def _final_norm(x, g, nb, r):
    b, l, d = x.shape
    blk = pl.BlockSpec((nb, r, d), lambda i, j: (i, j, 0))
    return pl.pallas_call(
        _final_body,
        out_shape=jax.ShapeDtypeStruct((b, l, d), F32),
        grid=(b // nb, l // r),
        in_specs=[blk, pl.BlockSpec((1, d), lambda i, j: (0, 0))],
        out_specs=blk,
        compiler_params=_cparams(("arbitrary", "arbitrary")),
        name="final_norm",
    )(x, g.reshape(1, d))


def _prep_w_in(w_in):
    wm = W_MIX
    o = 0
    w_conv = w_in[..., o:o + 2 * wm]
    o += 2 * wm
    n_ssm = 2 * wm + 2 * SSM_GROUPS * SSM_STATE
    w_ssm = w_in[..., o:o + n_ssm]
    w_dt = w_in[..., o + n_ssm:o + n_ssm + SSM_HEADS]
    o += n_ssm + SSM_HEADS
    w_dt = jnp.pad(w_dt, ((0, 0), (0, 0), (0, DT_PAD - SSM_HEADS)))
    w_ssm = jnp.concatenate([w_ssm, w_dt], axis=-1)

    def halves_first(w):
        s = w.shape[:-1]
        return w.reshape(*s, RET_HEADS, 2, RET_HALF).swapaxes(-2, -3).reshape(*s, RET_QK)

    w_q = halves_first(w_in[..., o:o + RET_QK])
    w_k = halves_first(w_in[..., o + RET_QK:o + 2 * RET_QK])
    w_ret = jnp.concatenate([w_q, w_k, w_in[..., o + 2 * RET_QK:o + 2 * RET_QK + 2 * wm]], axis=-1)
    o += 2 * RET_QK + 2 * wm
    w_pool = w_in[..., o:o + wm]
    return [w.astype(BF16) for w in (w_conv, w_ssm, w_ret, w_pool)]


def _pad_lanes(v, width, value=0.0):
    return jnp.pad(v, ((0, 0), (0, width - v.shape[-1])), constant_values=value)


def _group_plan(b, l):
    if l >= 256:
        return (1, 256), (256, CHUNK)
    nb = max(1, min(b, 256 // l))
    return (nb, l), (l, min(CHUNK, l))


def kernel(x_prompt, x_sample, c_prompt, c_sample, state_conv, state_ssm_conv, state_ssm, state_ret, state_pool,
           w_ada, b_ada, norm1_g, norm2_g, w_in, w_out, conv_w, conv_b, conv_ln_g, conv_ln_b,
           ssm_conv_w, ssm_conv_b, ssm_dt_bias, ssm_a_log, ssm_d, ssm_norm_g, pool_w, pool_scale,
           router_w, router_b, moe_w_gu, moe_b_gu, moe_w_down, moe_b_down, final_norm_g):
    depth = w_in.shape[0]
    d = x_prompt.shape[-1]
    bp, lp, _ = x_prompt.shape
    bs, ls, _ = x_sample.shape
    n_experts = router_w.shape[-1]
    wm = W_MIX
    hp = SSM_HEADS * SSM_HEADDIM
    hk = RET_HEADS * RET_KDIM

    cb = -(-(bp + bs) // 8) * 8
    c_all = jnp.concatenate([c_prompt, c_sample, jnp.zeros((cb - bp - bs, d), F32)], axis=0)
    mod_all = _ada(c_all, w_ada, b_ada)

    w_in_parts = _prep_w_in(w_in)
    w_out_b = w_out.astype(BF16)
    pool_w_b = pool_w.astype(BF16)
    r3 = lambda a: a.reshape(depth, 1, a.shape[-1])
    n1g, n2g = r3(norm1_g), r3(norm2_g)
    conv_b3, ln_g3, ln_b3 = r3(conv_b), r3(conv_ln_g), r3(conv_ln_b)
    scb3 = r3(ssm_conv_b)
    dtb3 = r3(_pad_lanes(ssm_dt_bias, DT_PAD))
    alog3 = r3(_pad_lanes(ssm_a_log, DT_PAD))
    dsk3 = r3(jnp.repeat(ssm_d, SSM_HEADDIM, axis=-1))
    sng3 = r3(ssm_norm_g)
    psc3 = r3(pool_scale)
    rb3 = r3(router_b)
    expander = (jnp.arange(DT_PAD)[:, None] == (jnp.arange(hp) // SSM_HEADDIM)[None, :]).astype(F32)

    groups = []
    for name, x, b, l, pos0 in (("p", x_prompt, bp, lp, 0), ("s", x_sample, bs, ls, PAST_LEN)):
        (nb, r), (rm, q) = _group_plan(b, l)
        groups.append(dict(name=name, x=x, b=b, l=l, pos0=pos0, nb=nb, r=r, rm=rm, q=q,
                           tables=_ret_tables(l, q, float(pos0)), new=([], [], [], [], [])))

    for layer in range(depth):
        mods = (mod_all[layer, :bp].reshape(bp, 1, -1), mod_all[layer, bp:bp + bs].reshape(bs, 1, -1))
        staged = []
        for gi, g in enumerate(groups):
            b, l = g["b"], g["l"]
            if g["name"] == "p":
                st_conv = jnp.zeros((b, CONV_HIST, wm), F32)
                st_sconv = jnp.zeros((b, SSD_HIST, SSM_XBC), F32)
                st_ssm = jnp.zeros((b, hp, SSM_STATE), F32)
                st_ret = jnp.zeros((b, hk, RET_VDIM), F32)
                st_pool = jnp.zeros((b, POOL_HIST, wm), F32)
            else:
                st_conv = jnp.pad(state_conv[layer], ((0, 0), (CONV_HIST - (CONV_K - 1), 0), (0, 0)))
                st_sconv = jnp.pad(state_ssm_conv[layer], ((0, 0), (SSD_HIST - (SSM_CONV_K - 1), 0), (0, 0)))
                st_ssm = state_ssm[layer].reshape(b, hp, SSM_STATE)
                st_ret = state_ret[layer].reshape(b, hk, RET_VDIM)
                st_pool = jnp.pad(state_pool[layer], ((0, 0), (POOL_HIST - POOL_BUF, 0), (0, 0)))
            mod = mods[gi]
            uc, us, ur, up = _inproj(g["x"], mod, n1g, w_in_parts, layer, g["nb"], g["r"])
            ya, n_conv = _conv_mixer(uc, st_conv, conv_w, conv_b3, ln_g3, ln_b3, layer, g["rm"])
            yb, n_ssm = _ssd_mixer(us, st_sconv, st_ssm, ssm_conv_w, scb3, dtb3, alog3, dsk3, sng3, expander,
                                   layer, g["rm"], g["q"])
            yc, n_ret = _ret_mixer(ur, st_ret, g["tables"], g["rm"], g["q"])
            yd = _pool_mixer(up, st_pool, pool_w_b, psc3, layer, g["rm"], g["pos0"])
            x1, h2, idx, gates = _outproj(g["x"], (ya, yb, yc, yd), mod, n2g, w_out_b, router_w, rb3,
                                          layer, g["nb"], g["r"])
            new = g["new"]
            new[0].append(n_conv[:, CONV_HIST - (CONV_K - 1):, :])
            new[1].append(us[:, l - (SSM_CONV_K - 1):, wm:wm + SSM_XBC])
            new[2].append(n_ssm.reshape(b, SSM_HEADS, SSM_HEADDIM, SSM_STATE))
            new[3].append(n_ret.reshape(b, RET_HEADS, RET_KDIM, RET_VDIM))
            new[4].append(up[:, l - POOL_BUF:, :])
            staged.append((x1, h2, idx, gates, mod))

        h2_all = jnp.concatenate([s[1].reshape(-1, d) for s in staged], axis=0)
        idx_all = jnp.concatenate([s[2].reshape(-1, TOP_K) for s in staged], axis=0)
        dest, row_token, sub_valid, sb_e, sb_nv, sb_blk, n_rows = _route(idx_all, n_experts)
        xbuf = _moe_gather(h2_all, row_token, sub_valid, n_rows)
        ybuf = _moe_experts(xbuf, sb_e, sb_nv, sb_blk, moe_w_gu, moe_b_gu, moe_w_down, moe_b_down, layer)
        off = 0
        for g, (x1, h2, idx, gates, mod) in zip(groups, staged):
            cnt = g["b"] * g["l"] * TOP_K
            g["x"] = _moe_combine(x1, mod, gates, dest[off:off + cnt], ybuf, g["nb"], g["r"])
            off += cnt

    outs = [_final_norm(g["x"], final_norm_g, g["nb"], g["r"]) for g in groups]
    states = []
    for g in groups:
        states.extend(jnp.stack(v, axis=0) for v in g["new"])
    return (outs[0], outs[1], *states)
```

```python
import functools
import math

import numpy as np
import jax
import jax.numpy as jnp
from jax import lax
from jax.experimental import pallas as pl
from jax.experimental.pallas import tpu as pltpu

F32 = jnp.float32
BF16 = jnp.bfloat16
HI = lax.Precision.HIGHEST

EPS = 1e-6
CHUNK = 64
PAST_LEN = 1024
CONV_K = 31
SSM_CONV_K = 4
SSM_HEADS = 8
SSM_HEADDIM = 64
SSM_GROUPS = 2
SSM_STATE = 128
RET_HEADS = 4
RET_KDIM = 64
RET_VDIM = 128
ROPE_BASE = 10000.0
POOL_WINDOWS = (2, 4, 8, 16)
POOL_BUF = 15
TOP_K = 4
SWIGLU_LIMIT = 7.0
SWIGLU_ALPHA = 1.702

W_MIX = 512
LANES = 128
DT_PAD = LANES
VMEM_LIMIT = 56 * 1024 * 1024

MOE_SUPER = 1024
MOE_SUB = 256
MOE_TF = 256


def _cparams(sem, vmem=VMEM_LIMIT):
    return pltpu.CompilerParams(dimension_semantics=sem, vmem_limit_bytes=vmem)


def _silu(x):
    return x * jax.nn.sigmoid(x)


def _ada_body(c_ref, w_ref, b_ref, o_ref):
    c = c_ref[...]
    o_ref[...] = jnp.dot(_silu(c), w_ref[...], preferred_element_type=F32, precision=HI) + b_ref[...]


def _ada(c_all, w_ada, b_ada):
    nl, d, n6 = w_ada.shape
    cb = c_all.shape[0]
    tn = 1024
    return pl.pallas_call(
        _ada_body,
        out_shape=jax.ShapeDtypeStruct((nl, cb, n6), F32),
        grid=(nl, n6 // tn),
        in_specs=[pl.BlockSpec((cb, d), lambda l, j: (0, 0)),
                  pl.BlockSpec((None, d, tn), lambda l, j: (l, 0, j)),
                  pl.BlockSpec((None, 1, tn), lambda l, j: (l, 0, j))],
        out_specs=pl.BlockSpec((None, cb, tn), lambda l, j: (l, 0, j)),
        compiler_params=_cparams(("arbitrary", "arbitrary")),
        name="ada",
    )(c_all, w_ada, b_ada.reshape(nl, 1, n6))


def _inproj_body(x_ref, sh_ref, sc_ref, g_ref, wc_ref, ws_ref, wr_ref, wp_ref, uc_ref, us_ref, ur_ref, up_ref):
    nb, r, d = x_ref.shape
    x = x_ref[...]
    ms = jnp.mean(x * x, axis=-1, keepdims=True)
    h = x * lax.rsqrt(ms + EPS) * g_ref[...]
    h = h * (1.0 + sc_ref[...]) + sh_ref[...]
    hb = h.reshape(nb * r, d).astype(BF16)
    for w_ref, u_ref in ((wc_ref, uc_ref), (ws_ref, us_ref), (wr_ref, ur_ref), (wp_ref, up_ref)):
        u = jnp.dot(hb, w_ref[...], preferred_element_type=F32)
        u_ref[...] = u.reshape(nb, r, u.shape[-1])


def _inproj(x, mod, norm_g, ws, layer, nb, r):
    b, l, d = x.shape
    grid = (b // nb, l // r)
    xs = pl.BlockSpec((nb, r, d), lambda i, j: (i, j, 0))
    wspecs = [pl.BlockSpec((None, d, w.shape[-1]), lambda i, j: (layer, 0, 0), pipeline_mode=pl.Buffered(1)) for w in ws]
    return pl.pallas_call(
        _inproj_body,
        out_shape=[jax.ShapeDtypeStruct((b, l, w.shape[-1]), F32) for w in ws],
        grid=grid,
        in_specs=[xs,
                  pl.BlockSpec((nb, 1, d), lambda i, j: (i, 0, 0)),
                  pl.BlockSpec((nb, 1, d), lambda i, j: (i, 0, 1)),
                  pl.BlockSpec((None, 1, d), lambda i, j: (layer, 0, 0))] + wspecs,
        out_specs=[pl.BlockSpec((nb, r, w.shape[-1]), lambda i, j: (i, j, 0)) for w in ws],
        compiler_params=_cparams(("arbitrary", "arbitrary")),
        name="inproj",
    )(x, mod, mod, norm_g, *ws)


CONV_HIST = 32
CONV_SUB = 64


def _conv_body(u_ref, st_ref, w_ref, b_ref, lg_ref, lb_ref, y_ref, ns_ref, hist_ref):
    r = u_ref.shape[0]
    wm = W_MIX

    @pl.when(pl.program_id(1) == 0)
    def _():
        hist_ref[0:CONV_HIST, :] = st_ref[...]

    u = u_ref[...]
    hist_ref[CONV_HIST:CONV_HIST + r, :] = u[:, :wm] * jax.nn.sigmoid(u[:, wm:])
    sub = min(CONV_SUB, r)
    for r0 in range(0, r, sub):
        acc = jnp.zeros((sub, wm), F32)
        for j in range(CONV_K):
            lo = CONV_HIST + r0 - j
            acc = acc + w_ref[CONV_K - 1 - j:CONV_K - j, :] * hist_ref[lo:lo + sub, :]
        acc = acc + b_ref[...]
        mu = jnp.mean(acc, axis=-1, keepdims=True)
        xc = acc - mu
        var = jnp.mean(xc * xc, axis=-1, keepdims=True)
        yn = xc * lax.rsqrt(var + EPS) * lg_ref[...] + lb_ref[...]
        y_ref[r0:r0 + sub, :] = _silu(yn).astype(y_ref.dtype)
    last = hist_ref[r:r + CONV_HIST, :]
    ns_ref[...] = last
    hist_ref[0:CONV_HIST, :] = last


def _conv_mixer(u, st, conv_w, conv_b, ln_g, ln_b, layer, r):
    b, l, _ = u.shape
    wm = W_MIX
    par = lambda k: pl.BlockSpec((None, k, wm), lambda i, j: (layer, 0, 0))
    return pl.pallas_call(
        _conv_body,
        out_shape=[jax.ShapeDtypeStruct((b, l, wm), BF16), jax.ShapeDtypeStruct((b, CONV_HIST, wm), F32)],
        grid=(b, l // r),
        in_specs=[pl.BlockSpec((None, r, 2 * wm), lambda i, j: (i, j, 0)),
                  pl.BlockSpec((None, CONV_HIST, wm), lambda i, j: (i, 0, 0)),
                  par(CONV_K), par(1), par(1), par(1)],
        out_specs=[pl.BlockSpec((None, r, wm), lambda i, j: (i, j, 0)),
                   pl.BlockSpec((None, CONV_HIST, wm), lambda i, j: (i, 0, 0))],
        scratch_shapes=[pltpu.VMEM((r + CONV_HIST, wm), F32)],
        compiler_params=_cparams(("arbitrary", "arbitrary")),
        name="conv_mixer",
    )(u, st, conv_w, conv_b, ln_g, ln_b)


POOL_HIST = 16


def _pool_body(p_ref, st_ref, w_ref, sc_ref, y_ref, hist_ref, *, pos0):
    r = p_ref.shape[0]
    i = pl.program_id(1)

    @pl.when(i == 0)
    def _():
        hist_ref[0:POOL_HIST, :] = st_ref[...]

    p = p_ref[...]
    hist_ref[POOL_HIST:POOL_HIST + r, :] = p
    pos = pos0 + i * r + lax.broadcasted_iota(jnp.int32, (r, 1), 0)
    outs = []
    for gi, w in enumerate(POOL_WINDOWS):
        lo = gi * LANES
        s = p[:, lo:lo + LANES]
        for j in range(1, w):
            s = s + hist_ref[POOL_HIST - j:POOL_HIST - j + r, lo:lo + LANES]
        cnt = jnp.minimum(pos + 1, w).astype(F32)
        pooled = s / cnt - p[:, lo:lo + LANES]
        outs.append(jnp.dot(pooled.astype(BF16), w_ref[gi], preferred_element_type=F32))
    y = jnp.concatenate(outs, axis=-1) * sc_ref[...]
    y_ref[...] = y.astype(y_ref.dtype)
    hist_ref[0:POOL_HIST, :] = hist_ref[r:r + POOL_HIST, :]


def _pool_mixer(p, st, pool_w, pool_scale, layer, r, pos0):
    b, l, wm = p.shape
    return pl.pallas_call(
        functools.partial(_pool_body, pos0=pos0),
        out_shape=jax.ShapeDtypeStruct((b, l, wm), BF16),
        grid=(b, l // r),
        in_specs=[pl.BlockSpec((None, r, wm), lambda i, j: (i, j, 0)),
                  pl.BlockSpec((None, POOL_HIST, wm), lambda i, j: (i, 0, 0)),
                  pl.BlockSpec((None, len(POOL_WINDOWS), LANES, LANES), lambda i, j: (layer, 0, 0, 0)),
                  pl.BlockSpec((None, 1, wm), lambda i, j: (layer, 0, 0))],
        out_specs=pl.BlockSpec((None, r, wm), lambda i, j: (i, j, 0)),
        scratch_shapes=[pltpu.VMEM((r + POOL_HIST, wm), F32)],
        compiler_params=_cparams(("arbitrary", "arbitrary")),
        name="pool_mixer",
    )(p, st, pool_w, pool_scale)


SSD_HIST = 8
SSM_XBC = W_MIX + 2 * SSM_GROUPS * SSM_STATE
GROUP_W = W_MIX // SSM_GROUPS


def _ssd_body(u_ref, cst_ref, hst_ref, cw_ref, cb_ref, dtb_ref, alog_ref, dsk_ref, ng_ref, ex_ref,
              y_ref, nh_ref, hist_ref, st_ref, *, q):
    r = u_ref.shape[0]
    wm = W_MIX
    n = SSM_STATE
    i = pl.program_id(1)

    @pl.when(i == 0)
    def _():
        hist_ref[0:SSD_HIST, :] = cst_ref[...]
        st_ref[...] = hst_ref[...].T

    hist_ref[SSD_HIST:SSD_HIST + r, :] = u_ref[:, wm:wm + SSM_XBC]
    a_neg = -jnp.exp(alog_ref[...])
    ex = ex_ref[...]
    row = lax.broadcasted_iota(jnp.int32, (q, q), 0)
    col = lax.broadcasted_iota(jnp.int32, (q, q), 1)
    tril = row >= col
    ltri = tril.astype(F32)
    eye = (lax.broadcasted_iota(jnp.int32, (LANES, LANES), 0)
           == lax.broadcasted_iota(jnp.int32, (LANES, LANES), 1)).astype(F32)
    head_of_lane = lax.broadcasted_iota(jnp.int32, (q, GROUP_W), 1) // SSM_HEADDIM
    heads_per_group = SSM_HEADS // SSM_GROUPS

    for c in range(r // q):
        r0 = c * q
        xc = jnp.zeros((q, SSM_XBC), F32)
        for k in range(SSM_CONV_K):
            lo = SSD_HIST + r0 + k - (SSM_CONV_K - 1)
            xc = xc + cw_ref[k:k + 1, :] * hist_ref[lo:lo + q, :]
        xc = _silu(xc + cb_ref[...])
        xs = xc[:, :wm]
        bm = xc[:, wm:wm + SSM_GROUPS * n]
        cm = xc[:, wm + SSM_GROUPS * n:]
        z = u_ref[r0:r0 + q, 0:wm]
        dt = jax.nn.softplus(u_ref[r0:r0 + q, wm + SSM_XBC:wm + SSM_XBC + DT_PAD] + dtb_ref[...])
        cum = jnp.dot(ltri, dt * a_neg, preferred_element_type=F32, precision=HI)
        cum_t = lax.dot_general(eye, cum, (((1,), (1,)), ((), ())), preferred_element_type=F32, precision=HI)
        dt_full = jnp.dot(dt, ex, preferred_element_type=F32, precision=HI)
        cum_full = jnp.dot(cum, ex, preferred_element_type=F32, precision=HI)
        cum_last = cum_full[q - 1:q, :]
        xdt = xs * dt_full
        xw = xdt * jnp.exp(cum_last - cum_full)
        ecum = jnp.exp(cum_full)
        ys = []
        for g in range(SSM_GROUPS):
            gl = slice(g * GROUP_W, (g + 1) * GROUP_W)
            bg = bm[:, g * n:(g + 1) * n].astype(BF16)
            cg = cm[:, g * n:(g + 1) * n].astype(BF16)
            cb = lax.dot_general(cg, bg, (((1,), (1,)), ((), ())), preferred_element_type=F32)
            xdt_g = xdt[:, gl].astype(BF16)
            yg = jnp.zeros((q, GROUP_W), F32)
            for hh in range(heads_per_group):
                h = g * heads_per_group + hh
                seg = cum_full[:, h * SSM_HEADDIM:h * SSM_HEADDIM + 1] - cum_t[h:h + 1, :]
                dec = jnp.exp(jnp.where(tril, seg, -jnp.inf))
                ph = jnp.dot((cb * dec).astype(BF16), xdt_g, preferred_element_type=F32)
                yg = jnp.where(head_of_lane == hh, ph, yg)
            st_g = st_ref[:, gl]
            yg = yg + ecum[:, gl] * jnp.dot(cg, st_g.astype(BF16), preferred_element_type=F32)
            upd = lax.dot_general(bg, xw[:, gl].astype(BF16), (((0,), (0,)), ((), ())), preferred_element_type=F32)
            st_ref[:, gl] = st_g * jnp.exp(cum_last[:, gl]) + upd
            ys.append(yg)
        y = jnp.concatenate(ys, axis=-1) + xs * dsk_ref[...]
        y = y * _silu(z)
        y = y * lax.rsqrt(jnp.mean(y * y, axis=-1, keepdims=True) + EPS) * ng_ref[...]
        y_ref[r0:r0 + q, :] = y.astype(y_ref.dtype)

    hist_ref[0:SSD_HIST, :] = hist_ref[r:r + SSD_HIST, :]

    @pl.when(i == pl.num_programs(1) - 1)
    def _():
        nh_ref[...] = st_ref[...].T


def _ssd_mixer(u, cst, hst, cw, cb, dtb, alog, dsk, ng, ex, layer, r, q):
    b, l, wu = u.shape
    wm = W_MIX
    hp = SSM_HEADS * SSM_HEADDIM
    par = lambda k, w: pl.BlockSpec((None, k, w), lambda i, j: (layer, 0, 0))
    return pl.pallas_call(
        functools.partial(_ssd_body, q=q),
        out_shape=[jax.ShapeDtypeStruct((b, l, wm), BF16), jax.ShapeDtypeStruct((b, hp, SSM_STATE), F32)],
        grid=(b, l // r),
        in_specs=[pl.BlockSpec((None, r, wu), lambda i, j: (i, j, 0)),
                  pl.BlockSpec((None, SSD_HIST, SSM_XBC), lambda i, j: (i, 0, 0)),
                  pl.BlockSpec((None, hp, SSM_STATE), lambda i, j: (i, 0, 0)),
                  par(SSM_CONV_K, SSM_XBC), par(1, SSM_XBC), par(1, DT_PAD), par(1, DT_PAD), par(1, wm), par(1, wm),
                  pl.BlockSpec((DT_PAD, wm), lambda i, j: (0, 0))],
        out_specs=[pl.BlockSpec((None, r, wm), lambda i, j: (i, j, 0)),
                   pl.BlockSpec((None, hp, SSM_STATE), lambda i, j: (i, 0, 0))],
        scratch_shapes=[pltpu.VMEM((r + SSD_HIST, SSM_XBC), F32), pltpu.VMEM((SSM_STATE, hp), F32)],
        compiler_params=_cparams(("arbitrary", "arbitrary")),
        name="ssd_mixer",
    )(u, cst, hst, cw, cb, dtb, alog, dsk, ng, ex)


RET_QK = RET_HEADS * RET_KDIM
RET_HALF = RET_KDIM // 2


def _ret_body(u_ref, st_ref, cos_ref, sin_ref, intra_ref, kdec_ref, qdec_ref, cdec_ref, y_ref, ns_ref, s_ref, *, q):
    r = u_ref.shape[0]
    wm = W_MIX
    i = pl.program_id(1)
    dv = RET_VDIM

    @pl.when(i == 0)
    def _():
        s_ref[...] = jnp.zeros_like(s_ref)
        for h in range(RET_HEADS):
            for b in range(2):
                s_ref[LANES * b + RET_HALF * h:LANES * b + RET_HALF * (h + 1), dv * h:dv * (h + 1)] = (
                    st_ref[RET_KDIM * h + RET_HALF * b:RET_KDIM * h + RET_HALF * (b + 1), :])

    head_of_qlane = (lax.broadcasted_iota(jnp.int32, (q, RET_QK), 1) % LANES) // RET_HALF
    bd_mask = ((lax.broadcasted_iota(jnp.int32, (RET_QK, wm), 0) % LANES) // RET_HALF
               == lax.broadcasted_iota(jnp.int32, (RET_QK, wm), 1) // dv)

    for c in range(r // q):
        r0 = c * q
        cos = cos_ref[r0:r0 + q, :]
        sin = sin_ref[r0:r0 + q, :]
        q1 = u_ref[r0:r0 + q, 0:LANES]
        q2 = u_ref[r0:r0 + q, LANES:2 * LANES]
        k1 = u_ref[r0:r0 + q, RET_QK:RET_QK + LANES]
        k2 = u_ref[r0:r0 + q, RET_QK + LANES:2 * RET_QK]
        v = u_ref[r0:r0 + q, 2 * RET_QK:2 * RET_QK + wm]
        gate = u_ref[r0:r0 + q, 2 * RET_QK + wm:2 * RET_QK + 2 * wm]
        qr = jnp.concatenate([q1 * cos - q2 * sin, q1 * sin + q2 * cos], axis=-1)
        kr = jnp.concatenate([k1 * cos - k2 * sin, k1 * sin + k2 * cos], axis=-1) * (RET_KDIM ** -0.5)
        qrb = qr.astype(BF16)
        krb = kr.astype(BF16)
        vb = v.astype(BF16)
        outs = []
        for h in range(RET_HEADS):
            qm = jnp.where(head_of_qlane == h, qr, 0.0).astype(BF16)
            sc = lax.dot_general(qm, krb, (((1,), (1,)), ((), ())), preferred_element_type=F32)
            ph = (sc * intra_ref[h]).astype(BF16)
            outs.append(jnp.dot(ph, vb[:, dv * h:dv * (h + 1)], preferred_element_type=F32))
        o = jnp.concatenate(outs, axis=-1)
        s_old = s_ref[...]
        o = o + qdec_ref[...] * jnp.dot(qrb, s_old.astype(BF16), preferred_element_type=F32)
        kv = lax.dot_general(krb, (v * kdec_ref[...]).astype(BF16), (((0,), (0,)), ((), ())), preferred_element_type=F32)
        s_ref[...] = s_old * cdec_ref[...] + jnp.where(bd_mask, kv, 0.0)
        normed = []
        for h in range(RET_HEADS):
            oh = o[:, dv * h:dv * (h + 1)]
            normed.append(oh * lax.rsqrt(jnp.mean(oh * oh, axis=-1, keepdims=True) + EPS))
        o = jnp.concatenate(normed, axis=-1) * _silu(gate)
        y_ref[r0:r0 + q, :] = o.astype(y_ref.dtype)

    @pl.when(i == pl.num_programs(1) - 1)
    def _():
        for h in range(RET_HEADS):
            for b in range(2):
                ns_ref[RET_KDIM * h + RET_HALF * b:RET_KDIM * h + RET_HALF * (b + 1), :] = (
                    s_ref[LANES * b + RET_HALF * h:LANES * b + RET_HALF * (h + 1), dv * h:dv * (h + 1)])


def _ret_tables(l, q, pos0):
    inv = 1.0 / (ROPE_BASE ** jnp.linspace(0.0, 1.0, RET_HALF, dtype=F32))
    pos = pos0 + jnp.arange(l, dtype=F32)
    ang = pos[:, None] * inv[None, :]
    cos = jnp.tile(jnp.cos(ang), (1, RET_HEADS))
    sin = jnp.tile(jnp.sin(ang), (1, RET_HEADS))
    log_gamma = jnp.log(1.0 - 2.0 ** (-5.0 - jnp.arange(RET_HEADS, dtype=F32)))
    idx = jnp.arange(q, dtype=F32)
    intra = jnp.exp(jnp.abs(idx[:, None] - idx[None, :])[None, :, :] * log_gamma[:, None, None])
    kdec = jnp.repeat(jnp.exp((q - idx)[:, None] * log_gamma), RET_VDIM, axis=1)
    qdec = jnp.repeat(jnp.exp(idx[:, None] * log_gamma), RET_VDIM, axis=1)
    cdec = jnp.repeat(jnp.exp(q * log_gamma), RET_VDIM)[None, :]
    return cos, sin, intra, kdec, qdec, cdec


def _ret_mixer(u, st, tables, r, q):
    b, l, wu = u.shape
    wm = W_MIX
    cos, sin, intra, kdec, qdec, cdec = tables
    hk = RET_HEADS * RET_KDIM
    full = lambda a: pl.BlockSpec(a.shape, lambda i, j: (0,) * a.ndim)
    return pl.pallas_call(
        functools.partial(_ret_body, q=q),
        out_shape=[jax.ShapeDtypeStruct((b, l, wm), BF16), jax.ShapeDtypeStruct((b, hk, RET_VDIM), F32)],
        grid=(b, l // r),
        in_specs=[pl.BlockSpec((None, r, wu), lambda i, j: (i, j, 0)),
                  pl.BlockSpec((None, hk, RET_VDIM), lambda i, j: (i, 0, 0)),
                  pl.BlockSpec((r, LANES), lambda i, j: (j, 0)),
                  pl.BlockSpec((r, LANES), lambda i, j: (j, 0)),
                  full(intra), full(kdec), full(qdec), full(cdec)],
        out_specs=[pl.BlockSpec((None, r, wm), lambda i, j: (i, j, 0)),
                   pl.BlockSpec((None, hk, RET_VDIM), lambda i, j: (i, 0, 0))],
        scratch_shapes=[pltpu.VMEM((RET_QK, wm), F32)],
        compiler_params=_cparams(("arbitrary", "arbitrary")),
        name="ret_mixer",
    )(u, st, cos, sin, intra, kdec, qdec, cdec)


def _outproj_body(x_ref, ya_ref, yb_ref, yc_ref, yd_ref, g1_ref, sh_ref, sc_ref, ng_ref, wo_ref, rwh_ref, rwl_ref,
                  rb_ref, *rest):
    x1_ref, h2_ref, idx_ref, gate_ref = rest[-4:]
    nb, r, d = x_ref.shape
    n = nb * r
    ycat = jnp.concatenate([ya_ref[...], yb_ref[...], yc_ref[...], yd_ref[...]], axis=-1).reshape(n, d)
    y = jnp.dot(ycat, wo_ref[...], preferred_element_type=F32).reshape(nb, r, d)
    x1 = x_ref[...] + g1_ref[...] * y
    x1_ref[...] = x1
    ms = jnp.mean(x1 * x1, axis=-1, keepdims=True)
    h2 = x1 * lax.rsqrt(ms + EPS) * ng_ref[...]
    h2 = (h2 * (1.0 + sc_ref[...]) + sh_ref[...]).reshape(n, d)
    h2_ref[...] = h2
    h2h = h2.astype(BF16)
    h2l = (h2 - h2h.astype(F32)).astype(BF16)
    rwh = rwh_ref[...]
    logits = (jnp.dot(h2h, rwh, preferred_element_type=F32) + jnp.dot(h2l, rwh, preferred_element_type=F32)
              + jnp.dot(h2h, rwl_ref[...], preferred_element_type=F32) + rb_ref[...])
    ne = logits.shape[-1]
    lane = lax.broadcasted_iota(jnp.int32, (n, ne), 1)
    vals, idxs = [], []
    for _ in range(TOP_K):
        m = jnp.max(logits, axis=-1, keepdims=True)
        idx = jnp.min(jnp.where(logits == m, lane, ne), axis=-1, keepdims=True)
        vals.append(m)
        idxs.append(idx)
        logits = jnp.where(lane == idx, -jnp.inf, logits)
    v = jnp.concatenate(vals, axis=-1)
    ex = jnp.exp(v - v[:, 0:1])
    gate_ref[...] = (ex / jnp.sum(ex, axis=-1, keepdims=True)).reshape(nb, r, TOP_K)
    idx_ref[...] = jnp.concatenate(idxs, axis=-1).reshape(nb, r, TOP_K)


def _outproj(x, ys, mod, norm_g, w_out, rw_hi, rw_lo, router_b, layer, nb, r, h2_shared, t_all, row0):
    b, l, d = x.shape
    ne = rw_hi.shape[-1]
    n = nb * r
    lt = l // r
    blk0 = row0 // n
    blk = lambda w: pl.BlockSpec((nb, r, w), lambda i, j: (i, j, 0))
    modspec = lambda k: pl.BlockSpec((nb, 1, d), lambda i, j: (i, 0, k))
    in_specs = [blk(d), blk(W_MIX), blk(W_MIX), blk(W_MIX), blk(W_MIX),
                modspec(2), modspec(3), modspec(4),
                pl.BlockSpec((None, 1, d), lambda i, j: (layer, 0, 0)),
                pl.BlockSpec((None, d, d), lambda i, j: (layer, 0, 0), pipeline_mode=pl.Buffered(1)),
                pl.BlockSpec((None, d, ne), lambda i, j: (layer, 0, 0)),
                pl.BlockSpec((None, d, ne), lambda i, j: (layer, 0, 0)),
                pl.BlockSpec((None, 1, ne), lambda i, j: (layer, 0, 0))]
    args = [x, *ys, mod, mod, mod, norm_g, w_out, rw_hi, rw_lo, router_b]
    aliases = {}
    if h2_shared is not None:
        in_specs.append(pl.BlockSpec(memory_space=pl.ANY))
        args.append(h2_shared)
        aliases = {len(args) - 1: 1}
    return pl.pallas_call(
        _outproj_body,
        out_shape=[jax.ShapeDtypeStruct((b, l, d), F32), jax.ShapeDtypeStruct((t_all, d), F32),
                   jax.ShapeDtypeStruct((b, l, TOP_K), jnp.int32), jax.ShapeDtypeStruct((b, l, TOP_K), F32)],
        grid=(b // nb, lt),
        in_specs=in_specs,
        out_specs=[blk(d), pl.BlockSpec((n, d), lambda i, j: (blk0 + i * lt + j, 0)), blk(TOP_K), blk(TOP_K)],
        input_output_aliases=aliases,
        compiler_params=_cparams(("arbitrary", "arbitrary")),
        name="outproj",
    )(*args)


MOE_SUBS = MOE_SUPER // MOE_SUB
MOE_DOWN_COLS = 512
GATHER_UNROLL = 8


def _expert_body(e_ref, nv_ref, blk_ref, tok_ref, tokn_ref, h_ref, wg_ref, wu_ref, bg_ref, bu_ref, wd_ref, bd_ref,
                 o_ref, xg_ref, xb_ref, sem):
    b = pl.program_id(0)
    j = pl.program_id(1)
    nv = nv_ref[b]
    nsub = (nv + MOE_SUB - 1) // MOE_SUB

    def row_copy(tref, r):
        return pltpu.make_async_copy(h_ref.at[pl.ds(tref[0, r], 1), :], xg_ref.at[pl.ds(r, 1), :], sem)

    def issue(tref, n_rows):
        def body(i, c):
            for k in range(GATHER_UNROLL):
                row_copy(tref, i * GATHER_UNROLL + k).start()
            return c
        lax.fori_loop(0, n_rows // GATHER_UNROLL, body, 0)

    @pl.when(j == 0)
    def _():
        @pl.when(b == 0)
        def _():
            issue(tok_ref, nsub * MOE_SUB)

        def drain(i, c):
            for k in range(GATHER_UNROLL):
                row_copy(tok_ref, i * GATHER_UNROLL + k).wait()
            return c
        lax.fori_loop(0, nsub * (MOE_SUB // GATHER_UNROLL), drain, 0)

        for s in range(MOE_SUBS):
            rows = slice(s * MOE_SUB, (s + 1) * MOE_SUB)

            @pl.when(s < nsub)
            def _():
                xb_ref[rows, :] = xg_ref[rows, :].astype(BF16)

        @pl.when(b + 1 < pl.num_programs(0))
        def _():
            nvn = nv_ref[jnp.minimum(b + 1, pl.num_programs(0) - 1)]
            issue(tokn_ref, (nvn + MOE_SUB - 1) // MOE_SUB * MOE_SUB)

    @pl.when(jnp.logical_and(j == 0, nv > 0))
    def _():
        o_ref[...] = jnp.broadcast_to(bd_ref[...], o_ref.shape)

    for n in range(1, MOE_SUBS + 1):
        m = n * MOE_SUB

        @pl.when(nsub == n)
        def _():
            x = xb_ref[0:m, :]
            g = jnp.dot(x, wg_ref[...].astype(BF16), preferred_element_type=F32) + bg_ref[...]
            u = jnp.dot(x, wu_ref[...].astype(BF16), preferred_element_type=F32) + bu_ref[...]
            g = jnp.minimum(g, SWIGLU_LIMIT)
            u = jnp.clip(u, -SWIGLU_LIMIT, SWIGLU_LIMIT)
            act = ((u + 1.0) * (g * jax.nn.sigmoid(SWIGLU_ALPHA * g))).astype(BF16)
            for c0 in range(0, o_ref.shape[-1], MOE_DOWN_COLS):
                cols = slice(c0, c0 + MOE_DOWN_COLS)
                o_ref[0:m, cols] += jnp.dot(act, wd_ref[:, cols].astype(BF16), preferred_element_type=F32)


def _moe_experts(h2, row_token, sb_e, sb_nv, sb_blk, w_gu, b_gu, w_down, b_down, layer):
    t, d = h2.shape
    n_sb = row_token.shape[0]
    n_rows = n_sb * MOE_SUPER
    nl, ne, _, f2 = w_gu.shape
    f = f2 // 2
    tf = min(MOE_TF, f)
    nj = f // tf

    def jj(b, j, nv):
        return jnp.where(nv[b] > 0, j, nj - 1)

    return pl.pallas_call(
        _expert_body,
        out_shape=jax.ShapeDtypeStruct((n_rows, d), F32),
        grid_spec=pltpu.PrefetchScalarGridSpec(
            num_scalar_prefetch=3,
            grid=(n_sb, nj),
            in_specs=[
                pl.BlockSpec((None, 1, MOE_SUPER), lambda b, j, e, nv, blk: (b, 0, 0), memory_space=pltpu.SMEM),
                pl.BlockSpec((None, 1, MOE_SUPER), lambda b, j, e, nv, blk: (jnp.minimum(b + 1, n_sb - 1), 0, 0),
                             memory_space=pltpu.SMEM),
                pl.BlockSpec(memory_space=pl.ANY),
                pl.BlockSpec((None, None, d, tf), lambda b, j, e, nv, blk: (layer, e[b], 0, jj(b, j, nv))),
                pl.BlockSpec((None, None, d, tf), lambda b, j, e, nv, blk: (layer, e[b], 0, nj + jj(b, j, nv))),
                pl.BlockSpec((None, None, 1, tf), lambda b, j, e, nv, blk: (layer, e[b], 0, jj(b, j, nv))),
                pl.BlockSpec((None, None, 1, tf), lambda b, j, e, nv, blk: (layer, e[b], 0, nj + jj(b, j, nv))),
                pl.BlockSpec((None, None, tf, d), lambda b, j, e, nv, blk: (layer, e[b], jj(b, j, nv), 0)),
                pl.BlockSpec((None, None, 1, d), lambda b, j, e, nv, blk: (layer, e[b], 0, 0)),
            ],
            out_specs=pl.BlockSpec((MOE_SUPER, d), lambda b, j, e, nv, blk: (blk[b], 0)),
            scratch_shapes=[pltpu.VMEM((MOE_SUPER, d), F32), pltpu.VMEM((MOE_SUPER, d), BF16),
                            pltpu.SemaphoreType.DMA(())]),
        compiler_params=_cparams(("arbitrary", "arbitrary")),
        name="moe_experts",
    )(sb_e, sb_nv, sb_blk, row_token, row_token, h2, w_gu, w_gu, b_gu.reshape(nl, ne, 1, f2),
      b_gu.reshape(nl, ne, 1, f2), w_down, b_down.reshape(nl, ne, 1, d))


def _combine_body(dest_ref, x_ref, g2_ref, gate_ref, y_ref, o_ref, gbuf_ref, sem):
    nb, r, d = x_ref.shape
    n = nb * r

    def issue(t, c):
        for k in range(TOP_K):
            pltpu.make_async_copy(y_ref.at[pl.ds(dest_ref[0, t * TOP_K + k], 1), :],
                                  gbuf_ref.at[k, pl.ds(t, 1), :], sem).start()
        return c

    lax.fori_loop(0, n, issue, 0)

    def drain(t, c):
        for k in range(TOP_K):
            pltpu.make_async_copy(y_ref.at[pl.ds(0, 1), :], gbuf_ref.at[k, pl.ds(t, 1), :], sem).wait()
        return c

    lax.fori_loop(0, n, drain, 0)
    gate = gate_ref[...].reshape(n, TOP_K)
    f = gbuf_ref[0] * gate[:, 0:1]
    for k in range(1, TOP_K):
        f = f + gbuf_ref[k] * gate[:, k:k + 1]
    o_ref[...] = x_ref[...] + g2_ref[...] * f.reshape(nb, r, d)


def _moe_combine(x1, mod, gates, dest, ybuf, nb, r):
    b, l, d = x1.shape
    n = nb * r
    nt = (b // nb) * (l // r)
    lt = l // r
    blk = lambda w: pl.BlockSpec((nb, r, w), lambda i, j: (i, j, 0))
    return pl.pallas_call(
        _combine_body,
        out_shape=jax.ShapeDtypeStruct((b, l, d), F32),
        grid=(b // nb, lt),
        in_specs=[pl.BlockSpec((None, 1, n * TOP_K), lambda i, j: (i * lt + j, 0, 0), memory_space=pltpu.SMEM),
                  blk(d),
                  pl.BlockSpec((nb, 1, d), lambda i, j: (i, 0, 5)),
                  blk(TOP_K),
                  pl.BlockSpec(memory_space=pl.ANY)],
        out_specs=blk(d),
        scratch_shapes=[pltpu.VMEM((TOP_K, n, d), F32), pltpu.SemaphoreType.DMA(())],
        compiler_params=_cparams(("arbitrary", "arbitrary")),
        name="moe_combine",
    )(dest.reshape(nt, 1, n * TOP_K), x1, mod, gates, ybuf)


def _route(idx_all, n_experts):
    m = idx_all.shape[0] * TOP_K
    e_flat = idx_all.reshape(m)
    onehot = (e_flat[:, None] == jnp.arange(n_experts, dtype=jnp.int32)[None, :]).astype(jnp.int32)
    csum = jnp.cumsum(onehot, axis=0)
    counts = csum[-1]
    rank = jnp.sum(csum * onehot, axis=1) - 1
    n_sb_e = (counts + MOE_SUPER - 1) // MOE_SUPER
    sb_end = jnp.cumsum(n_sb_e)
    sb_start = sb_end - n_sb_e
    dest = sb_start[e_flat] * MOE_SUPER + rank
    n_sb = -(-m // MOE_SUPER) + n_experts
    n_rows = n_sb * MOE_SUPER
    n_used = sb_end[-1]
    sb_id = jnp.arange(n_sb, dtype=jnp.int32)
    sb_clamped = jnp.minimum(sb_id, n_used - 1)
    sb_e = jnp.minimum(jnp.searchsorted(sb_end, sb_clamped, side='right'), n_experts - 1).astype(jnp.int32)
    within = sb_clamped - sb_start[sb_e]
    sb_nv = jnp.where(sb_id < n_used, jnp.clip(counts[sb_e] - within * MOE_SUPER, 0, MOE_SUPER), 0).astype(jnp.int32)
    row_token = jnp.zeros((n_rows,), jnp.int32).at[dest].set(jnp.arange(m, dtype=jnp.int32) // TOP_K)
    return (dest.astype(jnp.int32), row_token.reshape(n_sb, 1, MOE_SUPER), sb_e, sb_nv,
            sb_clamped.astype(jnp.int32))


def _final_body(x_ref, g_ref, o_ref):
    x = x_ref[...]
    o_ref[...] = x * lax.rsqrt(jnp.mean(x * x, axis=-1, keepdims=True) + EPS) * g_ref[...]


def _final_norm(x, g, nb, r):
    b, l, d = x.shape
    blk = pl.BlockSpec((nb, r, d), lambda i, j: (i, j, 0))
    return pl.pallas_call(
        _final_body,
        out_shape=jax.ShapeDtypeStruct((b, l, d), F32),
        grid=(b // nb, l // r),
        in_specs=[blk, pl.BlockSpec((1, d), lambda i, j: (0, 0))],
        out_specs=blk,
        compiler_params=_cparams(("arbitrary", "arbitrary")),
        name="final_norm",
    )(x, g.reshape(1, d))


def _prep_w_in(w_in):
    wm = W_MIX
    o = 0
    w_conv = w_in[..., o:o + 2 * wm]
    o += 2 * wm
    n_ssm = 2 * wm + 2 * SSM_GROUPS * SSM_STATE
    w_ssm = w_in[..., o:o + n_ssm]
    w_dt = w_in[..., o + n_ssm:o + n_ssm + SSM_HEADS]
    o += n_ssm + SSM_HEADS
    w_dt = jnp.pad(w_dt, ((0, 0), (0, 0), (0, DT_PAD - SSM_HEADS)))
    w_ssm = jnp.concatenate([w_ssm, w_dt], axis=-1)

    def halves_first(w):
        s = w.shape[:-1]
        return w.reshape(*s, RET_HEADS, 2, RET_HALF).swapaxes(-2, -3).reshape(*s, RET_QK)

    w_q = halves_first(w_in[..., o:o + RET_QK])
    w_k = halves_first(w_in[..., o + RET_QK:o + 2 * RET_QK])
    w_ret = jnp.concatenate([w_q, w_k, w_in[..., o + 2 * RET_QK:o + 2 * RET_QK + 2 * wm]], axis=-1)
    o += 2 * RET_QK + 2 * wm
    w_pool = w_in[..., o:o + wm]
    return [w.astype(BF16) for w in (w_conv, w_ssm, w_ret, w_pool)]


def _pad_lanes(v, width, value=0.0):
    return jnp.pad(v, ((0, 0), (0, width - v.shape[-1])), constant_values=value)


def _group_plan(b, l):
    if l >= 256:
        return (1, 256), (256, CHUNK)
    nb = max(1, min(b, 256 // l))
    return (nb, l), (l, min(CHUNK, l))


def kernel(x_prompt, x_sample, c_prompt, c_sample, state_conv, state_ssm_conv, state_ssm, state_ret, state_pool,
           w_ada, b_ada, norm1_g, norm2_g, w_in, w_out, conv_w, conv_b, conv_ln_g, conv_ln_b,
           ssm_conv_w, ssm_conv_b, ssm_dt_bias, ssm_a_log, ssm_d, ssm_norm_g, pool_w, pool_scale,
           router_w, router_b, moe_w_gu, moe_b_gu, moe_w_down, moe_b_down, final_norm_g):
    depth = w_in.shape[0]
    d = x_prompt.shape[-1]
    bp, lp, _ = x_prompt.shape
    bs, ls, _ = x_sample.shape
    n_experts = router_w.shape[-1]
    wm = W_MIX
    hp = SSM_HEADS * SSM_HEADDIM
    hk = RET_HEADS * RET_KDIM

    cb = -(-(bp + bs) // 8) * 8
    c_all = jnp.concatenate([c_prompt, c_sample, jnp.zeros((cb - bp - bs, d), F32)], axis=0)
    mod_all = _ada(c_all, w_ada, b_ada)

    w_in_parts = _prep_w_in(w_in)
    w_out_b = w_out.astype(BF16)
    pool_w_b = pool_w.astype(BF16)
    r3 = lambda a: a.reshape(depth, 1, a.shape[-1])
    n1g, n2g = r3(norm1_g), r3(norm2_g)
    conv_b3, ln_g3, ln_b3 = r3(conv_b), r3(conv_ln_g), r3(conv_ln_b)
    scb3 = r3(ssm_conv_b)
    dtb3 = r3(_pad_lanes(ssm_dt_bias, DT_PAD))
    alog3 = r3(_pad_lanes(ssm_a_log, DT_PAD))
    dsk3 = r3(jnp.repeat(ssm_d, SSM_HEADDIM, axis=-1))
    sng3 = r3(ssm_norm_g)
    psc3 = r3(pool_scale)
    rb3 = r3(router_b)
    rw_hi = router_w.astype(BF16)
    rw_lo = (router_w - rw_hi.astype(F32)).astype(BF16)
    t_all = bp * lp + bs * ls
    expander = (jnp.arange(DT_PAD)[:, None] == (jnp.arange(hp) // SSM_HEADDIM)[None, :]).astype(F32)

    groups = []
    for name, x, b, l, pos0 in (("p", x_prompt, bp, lp, 0), ("s", x_sample, bs, ls, PAST_LEN)):
        (nb, r), (rm, q) = _group_plan(b, l)
        groups.append(dict(name=name, x=x, b=b, l=l, pos0=pos0, nb=nb, r=r, rm=rm, q=q,
                           tables=_ret_tables(l, q, float(pos0)), new=([], [], [], [], [])))

    for layer in range(depth):
        mods = (mod_all[layer, :bp].reshape(bp, 1, -1), mod_all[layer, bp:bp + bs].reshape(bs, 1, -1))
        staged = []
        h2_all = None
        row0 = 0
        for gi, g in enumerate(groups):
            b, l = g["b"], g["l"]
            if g["name"] == "p":
                st_conv = jnp.zeros((b, CONV_HIST, wm), F32)
                st_sconv = jnp.zeros((b, SSD_HIST, SSM_XBC), F32)
                st_ssm = jnp.zeros((b, hp, SSM_STATE), F32)
                st_ret = jnp.zeros((b, hk, RET_VDIM), F32)
                st_pool = jnp.zeros((b, POOL_HIST, wm), F32)
            else:
                st_conv = jnp.pad(state_conv[layer], ((0, 0), (CONV_HIST - (CONV_K - 1), 0), (0, 0)))
                st_sconv = jnp.pad(state_ssm_conv[layer], ((0, 0), (SSD_HIST - (SSM_CONV_K - 1), 0), (0, 0)))
                st_ssm = state_ssm[layer].reshape(b, hp, SSM_STATE)
                st_ret = state_ret[layer].reshape(b, hk, RET_VDIM)
                st_pool = jnp.pad(state_pool[layer], ((0, 0), (POOL_HIST - POOL_BUF, 0), (0, 0)))
            mod = mods[gi]
            uc, us, ur, up = _inproj(g["x"], mod, n1g, w_in_parts, layer, g["nb"], g["r"])
            ya, n_conv = _conv_mixer(uc, st_conv, conv_w, conv_b3, ln_g3, ln_b3, layer, g["rm"])
            yb, n_ssm = _ssd_mixer(us, st_sconv, st_ssm, ssm_conv_w, scb3, dtb3, alog3, dsk3, sng3, expander,
                                   layer, g["rm"], g["q"])
            yc, n_ret = _ret_mixer(ur, st_ret, g["tables"], g["rm"], g["q"])
            yd = _pool_mixer(up, st_pool, pool_w_b, psc3, layer, g["rm"], g["pos0"])
            x1, h2_all, idx, gates = _outproj(g["x"], (ya, yb, yc, yd), mod, n2g, w_out_b, rw_hi, rw_lo, rb3,
                                              layer, g["nb"], g["r"], h2_all, t_all, row0)
            row0 += b * l
            new = g["new"]
            new[0].append(n_conv[:, CONV_HIST - (CONV_K - 1):, :])
            new[1].append(us[:, l - (SSM_CONV_K - 1):, wm:wm + SSM_XBC])
            new[2].append(n_ssm.reshape(b, SSM_HEADS, SSM_HEADDIM, SSM_STATE))
            new[3].append(n_ret.reshape(b, RET_HEADS, RET_KDIM, RET_VDIM))
            new[4].append(up[:, l - POOL_BUF:, :])
            staged.append((x1, idx, gates, mod))

        idx_all = jnp.concatenate([s[1].reshape(-1, TOP_K) for s in staged], axis=0)
        dest, row_token, sb_e, sb_nv, sb_blk = _route(idx_all, n_experts)
        ybuf = _moe_experts(h2_all, row_token, sb_e, sb_nv, sb_blk, moe_w_gu, moe_b_gu, moe_w_down, moe_b_down, layer)
        off = 0
        for g, (x1, idx, gates, mod) in zip(groups, staged):
            cnt = g["b"] * g["l"] * TOP_K
            g["x"] = _moe_combine(x1, mod, gates, dest[off:off + cnt], ybuf, g["nb"], g["r"])
            off += cnt

    outs = [_final_norm(g["x"], final_norm_g, g["nb"], g["r"]) for g in groups]
    states = []
    for g in groups:
        states.extend(jnp.stack(v, axis=0) for v in g["new"])
    return (outs[0], outs[1], *states)
```

```python
import functools
import math

import numpy as np
import jax
import jax.numpy as jnp
from jax import lax
from jax.experimental import pallas as pl
from jax.experimental.pallas import tpu as pltpu

F32 = jnp.float32
BF16 = jnp.bfloat16
HI = lax.Precision.HIGHEST

EPS = 1e-6
CHUNK = 64
PAST_LEN = 1024
CONV_K = 31
SSM_CONV_K = 4
SSM_HEADS = 8
SSM_HEADDIM = 64
SSM_GROUPS = 2
SSM_STATE = 128
RET_HEADS = 4
RET_KDIM = 64
RET_VDIM = 128
ROPE_BASE = 10000.0
POOL_WINDOWS = (2, 4, 8, 16)
POOL_BUF = 15
TOP_K = 4
SWIGLU_LIMIT = 7.0
SWIGLU_ALPHA = 1.702

W_MIX = 512
LANES = 128
DT_PAD = LANES
VMEM_LIMIT = 56 * 1024 * 1024

MOE_SUPER = 1024
MOE_SUB = 256
MOE_TF = 256


def _cparams(sem, vmem=VMEM_LIMIT):
    return pltpu.CompilerParams(dimension_semantics=sem, vmem_limit_bytes=vmem)


def _silu(x):
    return x * jax.nn.sigmoid(x)


def _ada_body(c_ref, w_ref, b_ref, o_ref):
    c = c_ref[...]
    o_ref[...] = jnp.dot(_silu(c), w_ref[...], preferred_element_type=F32, precision=HI) + b_ref[...]


def _ada(c_all, w_ada, b_ada):
    nl, d, n6 = w_ada.shape
    cb = c_all.shape[0]
    tn = 1024
    return pl.pallas_call(
        _ada_body,
        out_shape=jax.ShapeDtypeStruct((nl, cb, n6), F32),
        grid=(nl, n6 // tn),
        in_specs=[pl.BlockSpec((cb, d), lambda l, j: (0, 0)),
                  pl.BlockSpec((None, d, tn), lambda l, j: (l, 0, j)),
                  pl.BlockSpec((None, 1, tn), lambda l, j: (l, 0, j))],
        out_specs=pl.BlockSpec((None, cb, tn), lambda l, j: (l, 0, j)),
        compiler_params=_cparams(("arbitrary", "arbitrary")),
        name="ada",
    )(c_all, w_ada, b_ada.reshape(nl, 1, n6))


def _inproj_body(x_ref, sh_ref, sc_ref, g_ref, wc_ref, ws_ref, wr_ref, wp_ref, uc_ref, us_ref, ur_ref, up_ref):
    nb, r, d = x_ref.shape
    x = x_ref[...]
    ms = jnp.mean(x * x, axis=-1, keepdims=True)
    h = x * lax.rsqrt(ms + EPS) * g_ref[...]
    h = h * (1.0 + sc_ref[...]) + sh_ref[...]
    hb = h.reshape(nb * r, d).astype(BF16)
    for w_ref, u_ref in ((wc_ref, uc_ref), (ws_ref, us_ref), (wr_ref, ur_ref), (wp_ref, up_ref)):
        u = jnp.dot(hb, w_ref[...], preferred_element_type=F32)
        u_ref[...] = u.reshape(nb, r, u.shape[-1])


def _inproj(x, mod, norm_g, ws, layer, nb, r):
    b, l, d = x.shape
    grid = (b // nb, l // r)
    xs = pl.BlockSpec((nb, r, d), lambda i, j: (i, j, 0))
    wspecs = [pl.BlockSpec((None, d, w.shape[-1]), lambda i, j: (layer, 0, 0), pipeline_mode=pl.Buffered(1)) for w in ws]
    return pl.pallas_call(
        _inproj_body,
        out_shape=[jax.ShapeDtypeStruct((b, l, w.shape[-1]), F32) for w in ws],
        grid=grid,
        in_specs=[xs,
                  pl.BlockSpec((nb, 1, d), lambda i, j: (i, 0, 0)),
                  pl.BlockSpec((nb, 1, d), lambda i, j: (i, 0, 1)),
                  pl.BlockSpec((None, 1, d), lambda i, j: (layer, 0, 0))] + wspecs,
        out_specs=[pl.BlockSpec((nb, r, w.shape[-1]), lambda i, j: (i, j, 0)) for w in ws],
        compiler_params=_cparams(("arbitrary", "arbitrary")),
        name="inproj",
    )(x, mod, mod, norm_g, *ws)


CONV_HIST = 32
CONV_SUB = 64


def _conv_body(u_ref, st_ref, w_ref, b_ref, lg_ref, lb_ref, y_ref, ns_ref, hist_ref):
    r = u_ref.shape[0]
    wm = W_MIX

    @pl.when(pl.program_id(1) == 0)
    def _():
        hist_ref[0:CONV_HIST, :] = st_ref[...]

    u = u_ref[...]
    hist_ref[CONV_HIST:CONV_HIST + r, :] = u[:, :wm] * jax.nn.sigmoid(u[:, wm:])
    sub = min(CONV_SUB, r)
    for r0 in range(0, r, sub):
        acc = jnp.zeros((sub, wm), F32)
        for j in range(CONV_K):
            lo = CONV_HIST + r0 - j
            acc = acc + w_ref[CONV_K - 1 - j:CONV_K - j, :] * hist_ref[lo:lo + sub, :]
        acc = acc + b_ref[...]
        mu = jnp.mean(acc, axis=-1, keepdims=True)
        xc = acc - mu
        var = jnp.mean(xc * xc, axis=-1, keepdims=True)
        yn = xc * lax.rsqrt(var + EPS) * lg_ref[...] + lb_ref[...]
        y_ref[r0:r0 + sub, :] = _silu(yn).astype(y_ref.dtype)
    last = hist_ref[r:r + CONV_HIST, :]
    ns_ref[...] = last
    hist_ref[0:CONV_HIST, :] = last


def _conv_mixer(u, st, conv_w, conv_b, ln_g, ln_b, layer, r):
    b, l, _ = u.shape
    wm = W_MIX
    par = lambda k: pl.BlockSpec((None, k, wm), lambda i, j: (layer, 0, 0))
    return pl.pallas_call(
        _conv_body,
        out_shape=[jax.ShapeDtypeStruct((b, l, wm), BF16), jax.ShapeDtypeStruct((b, CONV_HIST, wm), F32)],
        grid=(b, l // r),
        in_specs=[pl.BlockSpec((None, r, 2 * wm), lambda i, j: (i, j, 0)),
                  pl.BlockSpec((None, CONV_HIST, wm), lambda i, j: (i, 0, 0)),
                  par(CONV_K), par(1), par(1), par(1)],
        out_specs=[pl.BlockSpec((None, r, wm), lambda i, j: (i, j, 0)),
                   pl.BlockSpec((None, CONV_HIST, wm), lambda i, j: (i, 0, 0))],
        scratch_shapes=[pltpu.VMEM((r + CONV_HIST, wm), F32)],
        compiler_params=_cparams(("arbitrary", "arbitrary")),
        name="conv_mixer",
    )(u, st, conv_w, conv_b, ln_g, ln_b)


POOL_HIST = 16


def _pool_body(p_ref, st_ref, w_ref, sc_ref, y_ref, hist_ref, *, pos0):
    r = p_ref.shape[0]
    i = pl.program_id(1)

    @pl.when(i == 0)
    def _():
        hist_ref[0:POOL_HIST, :] = st_ref[...]

    p = p_ref[...]
    hist_ref[POOL_HIST:POOL_HIST + r, :] = p
    pos = pos0 + i * r + lax.broadcasted_iota(jnp.int32, (r, 1), 0)
    outs = []
    for gi, w in enumerate(POOL_WINDOWS):
        lo = gi * LANES
        s = p[:, lo:lo + LANES]
        for j in range(1, w):
            s = s + hist_ref[POOL_HIST - j:POOL_HIST - j + r, lo:lo + LANES]
        cnt = jnp.minimum(pos + 1, w).astype(F32)
        pooled = s / cnt - p[:, lo:lo + LANES]
        outs.append(jnp.dot(pooled.astype(BF16), w_ref[gi], preferred_element_type=F32))
    y = jnp.concatenate(outs, axis=-1) * sc_ref[...]
    y_ref[...] = y.astype(y_ref.dtype)
    hist_ref[0:POOL_HIST, :] = hist_ref[r:r + POOL_HIST, :]


def _pool_mixer(p, st, pool_w, pool_scale, layer, r, pos0):
    b, l, wm = p.shape
    return pl.pallas_call(
        functools.partial(_pool_body, pos0=pos0),
        out_shape=jax.ShapeDtypeStruct((b, l, wm), BF16),
        grid=(b, l // r),
        in_specs=[pl.BlockSpec((None, r, wm), lambda i, j: (i, j, 0)),
                  pl.BlockSpec((None, POOL_HIST, wm), lambda i, j: (i, 0, 0)),
                  pl.BlockSpec((None, len(POOL_WINDOWS), LANES, LANES), lambda i, j: (layer, 0, 0, 0)),
                  pl.BlockSpec((None, 1, wm), lambda i, j: (layer, 0, 0))],
        out_specs=pl.BlockSpec((None, r, wm), lambda i, j: (i, j, 0)),
        scratch_shapes=[pltpu.VMEM((r + POOL_HIST, wm), F32)],
        compiler_params=_cparams(("arbitrary", "arbitrary")),
        name="pool_mixer",
    )(p, st, pool_w, pool_scale)


SSD_HIST = 8
SSM_XBC = W_MIX + 2 * SSM_GROUPS * SSM_STATE
GROUP_W = W_MIX // SSM_GROUPS


def _ssd_body(u_ref, cst_ref, hst_ref, cw_ref, cb_ref, dtb_ref, alog_ref, dsk_ref, ng_ref, ex_ref,
              y_ref, nh_ref, hist_ref, st_ref, *, q):
    r = u_ref.shape[0]
    wm = W_MIX
    n = SSM_STATE
    i = pl.program_id(1)

    @pl.when(i == 0)
    def _():
        hist_ref[0:SSD_HIST, :] = cst_ref[...]
        st_ref[...] = hst_ref[...].T

    hist_ref[SSD_HIST:SSD_HIST + r, :] = u_ref[:, wm:wm + SSM_XBC]
    a_neg = -jnp.exp(alog_ref[...])
    ex = ex_ref[...]
    row = lax.broadcasted_iota(jnp.int32, (q, q), 0)
    col = lax.broadcasted_iota(jnp.int32, (q, q), 1)
    tril = row >= col
    ltri = tril.astype(F32)
    eye = (lax.broadcasted_iota(jnp.int32, (LANES, LANES), 0)
           == lax.broadcasted_iota(jnp.int32, (LANES, LANES), 1)).astype(F32)
    head_of_lane = lax.broadcasted_iota(jnp.int32, (q, GROUP_W), 1) // SSM_HEADDIM
    heads_per_group = SSM_HEADS // SSM_GROUPS

    for c in range(r // q):
        r0 = c * q
        xc = jnp.zeros((q, SSM_XBC), F32)
        for k in range(SSM_CONV_K):
            lo = SSD_HIST + r0 + k - (SSM_CONV_K - 1)
            xc = xc + cw_ref[k:k + 1, :] * hist_ref[lo:lo + q, :]
        xc = _silu(xc + cb_ref[...])
        xs = xc[:, :wm]
        bm = xc[:, wm:wm + SSM_GROUPS * n]
        cm = xc[:, wm + SSM_GROUPS * n:]
        z = u_ref[r0:r0 + q, 0:wm]
        dt = jax.nn.softplus(u_ref[r0:r0 + q, wm + SSM_XBC:wm + SSM_XBC + DT_PAD] + dtb_ref[...])
        cum = jnp.dot(ltri, dt * a_neg, preferred_element_type=F32, precision=HI)
        cum_t = lax.dot_general(eye, cum, (((1,), (1,)), ((), ())), preferred_element_type=F32, precision=HI)
        dt_full = jnp.dot(dt, ex, preferred_element_type=F32, precision=HI)
        cum_full = jnp.dot(cum, ex, preferred_element_type=F32, precision=HI)
        cum_last = cum_full[q - 1:q, :]
        xdt = xs * dt_full
        xw = xdt * jnp.exp(cum_last - cum_full)
        ecum = jnp.exp(cum_full)
        ys = []
        for g in range(SSM_GROUPS):
            gl = slice(g * GROUP_W, (g + 1) * GROUP_W)
            bg = bm[:, g * n:(g + 1) * n].astype(BF16)
            cg = cm[:, g * n:(g + 1) * n].astype(BF16)
            cb = lax.dot_general(cg, bg, (((1,), (1,)), ((), ())), preferred_element_type=F32)
            xdt_g = xdt[:, gl].astype(BF16)
            yg = jnp.zeros((q, GROUP_W), F32)
            for hh in range(heads_per_group):
                h = g * heads_per_group + hh
                seg = cum_full[:, h * SSM_HEADDIM:h * SSM_HEADDIM + 1] - cum_t[h:h + 1, :]
                dec = jnp.exp(jnp.where(tril, seg, -jnp.inf))
                ph = jnp.dot((cb * dec).astype(BF16), xdt_g, preferred_element_type=F32)
                yg = jnp.where(head_of_lane == hh, ph, yg)
            st_g = st_ref[:, gl]
            yg = yg + ecum[:, gl] * jnp.dot(cg, st_g.astype(BF16), preferred_element_type=F32)
            upd = lax.dot_general(bg, xw[:, gl].astype(BF16), (((0,), (0,)), ((), ())), preferred_element_type=F32)
            st_ref[:, gl] = st_g * jnp.exp(cum_last[:, gl]) + upd
            ys.append(yg)
        y = jnp.concatenate(ys, axis=-1) + xs * dsk_ref[...]
        y = y * _silu(z)
        y = y * lax.rsqrt(jnp.mean(y * y, axis=-1, keepdims=True) + EPS) * ng_ref[...]
        y_ref[r0:r0 + q, :] = y.astype(y_ref.dtype)

    hist_ref[0:SSD_HIST, :] = hist_ref[r:r + SSD_HIST, :]

    @pl.when(i == pl.num_programs(1) - 1)
    def _():
        nh_ref[...] = st_ref[...].T


def _ssd_mixer(u, cst, hst, cw, cb, dtb, alog, dsk, ng, ex, layer, r, q):
    b, l, wu = u.shape
    wm = W_MIX
    hp = SSM_HEADS * SSM_HEADDIM
    par = lambda k, w: pl.BlockSpec((None, k, w), lambda i, j: (layer, 0, 0))
    return pl.pallas_call(
        functools.partial(_ssd_body, q=q),
        out_shape=[jax.ShapeDtypeStruct((b, l, wm), BF16), jax.ShapeDtypeStruct((b, hp, SSM_STATE), F32)],
        grid=(b, l // r),
        in_specs=[pl.BlockSpec((None, r, wu), lambda i, j: (i, j, 0)),
                  pl.BlockSpec((None, SSD_HIST, SSM_XBC), lambda i, j: (i, 0, 0)),
                  pl.BlockSpec((None, hp, SSM_STATE), lambda i, j: (i, 0, 0)),
                  par(SSM_CONV_K, SSM_XBC), par(1, SSM_XBC), par(1, DT_PAD), par(1, DT_PAD), par(1, wm), par(1, wm),
                  pl.BlockSpec((DT_PAD, wm), lambda i, j: (0, 0))],
        out_specs=[pl.BlockSpec((None, r, wm), lambda i, j: (i, j, 0)),
                   pl.BlockSpec((None, hp, SSM_STATE), lambda i, j: (i, 0, 0))],
        scratch_shapes=[pltpu.VMEM((r + SSD_HIST, SSM_XBC), F32), pltpu.VMEM((SSM_STATE, hp), F32)],
        compiler_params=_cparams(("arbitrary", "arbitrary")),
        name="ssd_mixer",
    )(u, cst, hst, cw, cb, dtb, alog, dsk, ng, ex)


RET_QK = RET_HEADS * RET_KDIM
RET_HALF = RET_KDIM // 2


def _ret_body(u_ref, st_ref, cos_ref, sin_ref, intra_ref, kdec_ref, qdec_ref, cdec_ref, y_ref, ns_ref, s_ref, *, q):
    r = u_ref.shape[0]
    wm = W_MIX
    i = pl.program_id(1)
    dv = RET_VDIM

    @pl.when(i == 0)
    def _():
        s_ref[...] = jnp.zeros_like(s_ref)
        for h in range(RET_HEADS):
            for b in range(2):
                s_ref[LANES * b + RET_HALF * h:LANES * b + RET_HALF * (h + 1), dv * h:dv * (h + 1)] = (
                    st_ref[RET_KDIM * h + RET_HALF * b:RET_KDIM * h + RET_HALF * (b + 1), :])

    head_of_qlane = (lax.broadcasted_iota(jnp.int32, (q, RET_QK), 1) % LANES) // RET_HALF
    bd_mask = ((lax.broadcasted_iota(jnp.int32, (RET_QK, wm), 0) % LANES) // RET_HALF
               == lax.broadcasted_iota(jnp.int32, (RET_QK, wm), 1) // dv)

    for c in range(r // q):
        r0 = c * q
        cos = cos_ref[r0:r0 + q, :]
        sin = sin_ref[r0:r0 + q, :]
        q1 = u_ref[r0:r0 + q, 0:LANES]
        q2 = u_ref[r0:r0 + q, LANES:2 * LANES]
        k1 = u_ref[r0:r0 + q, RET_QK:RET_QK + LANES]
        k2 = u_ref[r0:r0 + q, RET_QK + LANES:2 * RET_QK]
        v = u_ref[r0:r0 + q, 2 * RET_QK:2 * RET_QK + wm]
        gate = u_ref[r0:r0 + q, 2 * RET_QK + wm:2 * RET_QK + 2 * wm]
        qr = jnp.concatenate([q1 * cos - q2 * sin, q1 * sin + q2 * cos], axis=-1)
        kr = jnp.concatenate([k1 * cos - k2 * sin, k1 * sin + k2 * cos], axis=-1) * (RET_KDIM ** -0.5)
        qrb = qr.astype(BF16)
        krb = kr.astype(BF16)
        vb = v.astype(BF16)
        outs = []
        for h in range(RET_HEADS):
            qm = jnp.where(head_of_qlane == h, qr, 0.0).astype(BF16)
            sc = lax.dot_general(qm, krb, (((1,), (1,)), ((), ())), preferred_element_type=F32)
            ph = (sc * intra_ref[h]).astype(BF16)
            outs.append(jnp.dot(ph, vb[:, dv * h:dv * (h + 1)], preferred_element_type=F32))
        o = jnp.concatenate(outs, axis=-1)
        s_old = s_ref[...]
        o = o + qdec_ref[...] * jnp.dot(qrb, s_old.astype(BF16), preferred_element_type=F32)
        kv = lax.dot_general(krb, (v * kdec_ref[...]).astype(BF16), (((0,), (0,)), ((), ())), preferred_element_type=F32)
        s_ref[...] = s_old * cdec_ref[...] + jnp.where(bd_mask, kv, 0.0)
        normed = []
        for h in range(RET_HEADS):
            oh = o[:, dv * h:dv * (h + 1)]
            normed.append(oh * lax.rsqrt(jnp.mean(oh * oh, axis=-1, keepdims=True) + EPS))
        o = jnp.concatenate(normed, axis=-1) * _silu(gate)
        y_ref[r0:r0 + q, :] = o.astype(y_ref.dtype)

    @pl.when(i == pl.num_programs(1) - 1)
    def _():
        for h in range(RET_HEADS):
            for b in range(2):
                ns_ref[RET_KDIM * h + RET_HALF * b:RET_KDIM * h + RET_HALF * (b + 1), :] = (
                    s_ref[LANES * b + RET_HALF * h:LANES * b + RET_HALF * (h + 1), dv * h:dv * (h + 1)])


def _ret_tables(l, q, pos0):
    inv = 1.0 / (ROPE_BASE ** jnp.linspace(0.0, 1.0, RET_HALF, dtype=F32))
    pos = pos0 + jnp.arange(l, dtype=F32)
    ang = pos[:, None] * inv[None, :]
    cos = jnp.tile(jnp.cos(ang), (1, RET_HEADS))
    sin = jnp.tile(jnp.sin(ang), (1, RET_HEADS))
    log_gamma = jnp.log(1.0 - 2.0 ** (-5.0 - jnp.arange(RET_HEADS, dtype=F32)))
    idx = jnp.arange(q, dtype=F32)
    intra = jnp.exp(jnp.abs(idx[:, None] - idx[None, :])[None, :, :] * log_gamma[:, None, None])
    kdec = jnp.repeat(jnp.exp((q - idx)[:, None] * log_gamma), RET_VDIM, axis=1)
    qdec = jnp.repeat(jnp.exp(idx[:, None] * log_gamma), RET_VDIM, axis=1)
    cdec = jnp.repeat(jnp.exp(q * log_gamma), RET_VDIM)[None, :]
    return cos, sin, intra, kdec, qdec, cdec


def _ret_mixer(u, st, tables, r, q):
    b, l, wu = u.shape
    wm = W_MIX
    cos, sin, intra, kdec, qdec, cdec = tables
    hk = RET_HEADS * RET_KDIM
    full = lambda a: pl.BlockSpec(a.shape, lambda i, j: (0,) * a.ndim)
    return pl.pallas_call(
        functools.partial(_ret_body, q=q),
        out_shape=[jax.ShapeDtypeStruct((b, l, wm), BF16), jax.ShapeDtypeStruct((b, hk, RET_VDIM), F32)],
        grid=(b, l // r),
        in_specs=[pl.BlockSpec((None, r, wu), lambda i, j: (i, j, 0)),
                  pl.BlockSpec((None, hk, RET_VDIM), lambda i, j: (i, 0, 0)),
                  pl.BlockSpec((r, LANES), lambda i, j: (j, 0)),
                  pl.BlockSpec((r, LANES), lambda i, j: (j, 0)),
                  full(intra), full(kdec), full(qdec), full(cdec)],
        out_specs=[pl.BlockSpec((None, r, wm), lambda i, j: (i, j, 0)),
                   pl.BlockSpec((None, hk, RET_VDIM), lambda i, j: (i, 0, 0))],
        scratch_shapes=[pltpu.VMEM((RET_QK, wm), F32)],
        compiler_params=_cparams(("arbitrary", "arbitrary")),
        name="ret_mixer",
    )(u, st, cos, sin, intra, kdec, qdec, cdec)


def _outproj_body(x_ref, ya_ref, yb_ref, yc_ref, yd_ref, g1_ref, sh_ref, sc_ref, ng_ref, wo_ref, rwh_ref, rwl_ref,
                  rb_ref, *rest):
    x1_ref, h2_ref, idx_ref, gate_ref = rest[-4:]
    nb, r, d = x_ref.shape
    n = nb * r
    ycat = jnp.concatenate([ya_ref[...], yb_ref[...], yc_ref[...], yd_ref[...]], axis=-1).reshape(n, d)
    y = jnp.dot(ycat, wo_ref[...], preferred_element_type=F32).reshape(nb, r, d)
    x1 = x_ref[...] + g1_ref[...] * y
    x1_ref[...] = x1
    ms = jnp.mean(x1 * x1, axis=-1, keepdims=True)
    h2 = x1 * lax.rsqrt(ms + EPS) * ng_ref[...]
    h2 = (h2 * (1.0 + sc_ref[...]) + sh_ref[...]).reshape(n, d)
    h2_ref[...] = h2
    h2h = h2.astype(BF16)
    h2l = (h2 - h2h.astype(F32)).astype(BF16)
    rwh = rwh_ref[...]
    logits = (jnp.dot(h2h, rwh, preferred_element_type=F32) + jnp.dot(h2l, rwh, preferred_element_type=F32)
              + jnp.dot(h2h, rwl_ref[...], preferred_element_type=F32) + rb_ref[...])
    ne = logits.shape[-1]
    lane = lax.broadcasted_iota(jnp.int32, (n, ne), 1)
    vals, idxs = [], []
    for _ in range(TOP_K):
        m = jnp.max(logits, axis=-1, keepdims=True)
        idx = jnp.min(jnp.where(logits == m, lane, ne), axis=-1, keepdims=True)
        vals.append(m)
        idxs.append(idx)
        logits = jnp.where(lane == idx, -jnp.inf, logits)
    v = jnp.concatenate(vals, axis=-1)
    ex = jnp.exp(v - v[:, 0:1])
    gate_ref[...] = (ex / jnp.sum(ex, axis=-1, keepdims=True)).reshape(nb, r, TOP_K)
    idx_ref[...] = jnp.concatenate(idxs, axis=-1).reshape(nb, r, TOP_K)


def _outproj(x, ys, mod, norm_g, w_out, rw_hi, rw_lo, router_b, layer, nb, r, h2_shared, t_all, row0):
    b, l, d = x.shape
    ne = rw_hi.shape[-1]
    n = nb * r
    lt = l // r
    blk0 = row0 // n
    blk = lambda w: pl.BlockSpec((nb, r, w), lambda i, j: (i, j, 0))
    modspec = lambda k: pl.BlockSpec((nb, 1, d), lambda i, j: (i, 0, k))
    in_specs = [blk(d), blk(W_MIX), blk(W_MIX), blk(W_MIX), blk(W_MIX),
                modspec(2), modspec(3), modspec(4),
                pl.BlockSpec((None, 1, d), lambda i, j: (layer, 0, 0)),
                pl.BlockSpec((None, d, d), lambda i, j: (layer, 0, 0), pipeline_mode=pl.Buffered(1)),
                pl.BlockSpec((None, d, ne), lambda i, j: (layer, 0, 0)),
                pl.BlockSpec((None, d, ne), lambda i, j: (layer, 0, 0)),
                pl.BlockSpec((None, 1, ne), lambda i, j: (layer, 0, 0))]
    args = [x, *ys, mod, mod, mod, norm_g, w_out, rw_hi, rw_lo, router_b]
    aliases = {}
    if h2_shared is not None:
        in_specs.append(pl.BlockSpec(memory_space=pl.ANY))
        args.append(h2_shared)
        aliases = {len(args) - 1: 1}
    return pl.pallas_call(
        _outproj_body,
        out_shape=[jax.ShapeDtypeStruct((b, l, d), F32), jax.ShapeDtypeStruct((t_all, d), F32),
                   jax.ShapeDtypeStruct((b, l, TOP_K), jnp.int32), jax.ShapeDtypeStruct((b, l, TOP_K), F32)],
        grid=(b // nb, lt),
        in_specs=in_specs,
        out_specs=[blk(d), pl.BlockSpec((n, d), lambda i, j: (blk0 + i * lt + j, 0)), blk(TOP_K), blk(TOP_K)],
        input_output_aliases=aliases,
        compiler_params=_cparams(("arbitrary", "arbitrary")),
        name="outproj",
    )(*args)


MOE_SUBS = MOE_SUPER // MOE_SUB
MOE_DOWN_COLS = 512
GATHER_UNROLL = 8


def _expert_body(e_ref, nv_ref, blk_ref, tok_ref, tokn_ref, h_ref, wg_ref, wu_ref, bg_ref, bu_ref, wd_ref, bd_ref,
                 o_ref, xg_ref, xb_ref, sem, *, nj):
    b = pl.program_id(0)
    j = pl.program_id(1)
    nv = nv_ref[b]
    nsub = (nv + MOE_SUB - 1) // MOE_SUB
    per_step = MOE_SUPER // nj
    n_phase = 2 + o_ref.shape[-1] // MOE_DOWN_COLS
    per_phase = per_step // n_phase + (1 if per_step % n_phase else 0)

    def row_copy(tref, r):
        return pltpu.make_async_copy(h_ref.at[pl.ds(tref[0, r], 1), :], xg_ref.at[pl.ds(r, 1), :], sem)

    def issue_share(phase):
        base = j * per_step
        for k in range(phase * per_phase, min((phase + 1) * per_phase, per_step)):
            row_copy(tokn_ref, base + k).start()

    @pl.when(j == 0)
    def _():
        @pl.when(b == 0)
        def _():
            def body(i, c):
                for k in range(GATHER_UNROLL):
                    row_copy(tok_ref, i * GATHER_UNROLL + k).start()
                return c
            lax.fori_loop(0, MOE_SUPER // GATHER_UNROLL, body, 0)

        prev_issued = jnp.logical_or(b == 0, nv_ref[jnp.maximum(b - 1, 0)] > 0)

        @pl.when(prev_issued)
        def _():
            pltpu.make_async_copy(h_ref.at[pl.ds(0, MOE_SUPER), :], xg_ref, sem).wait()

        for s in range(MOE_SUBS):
            rows = slice(s * MOE_SUB, (s + 1) * MOE_SUB)

            @pl.when(s < nsub)
            def _():
                xb_ref[rows, :] = xg_ref[rows, :].astype(BF16)

        @pl.when(nv > 0)
        def _():
            o_ref[...] = jnp.broadcast_to(bd_ref[...], o_ref.shape)

    for n in range(1, MOE_SUBS + 1):
        m = n * MOE_SUB

        @pl.when(nsub == n)
        def _():
            x = xb_ref[0:m, :]
            issue_share(0)
            g = jnp.dot(x, wg_ref[...].astype(BF16), preferred_element_type=F32) + bg_ref[...]
            issue_share(1)
            u = jnp.dot(x, wu_ref[...].astype(BF16), preferred_element_type=F32) + bu_ref[...]
            g = jnp.minimum(g, SWIGLU_LIMIT)
            u = jnp.clip(u, -SWIGLU_LIMIT, SWIGLU_LIMIT)
            act = ((u + 1.0) * (g * jax.nn.sigmoid(SWIGLU_ALPHA * g))).astype(BF16)
            for ci, c0 in enumerate(range(0, o_ref.shape[-1], MOE_DOWN_COLS)):
                cols = slice(c0, c0 + MOE_DOWN_COLS)
                issue_share(2 + ci)
                o_ref[0:m, cols] += jnp.dot(act, wd_ref[:, cols].astype(BF16), preferred_element_type=F32)


def _moe_experts(h2, row_token, sb_e, sb_nv, sb_blk, w_gu, b_gu, w_down, b_down, layer):
    t, d = h2.shape
    n_sb = row_token.shape[0]
    n_rows = n_sb * MOE_SUPER
    nl, ne, _, f2 = w_gu.shape
    f = f2 // 2
    tf = min(MOE_TF, f)
    nj = f // tf

    def jj(b, j, nv):
        return jnp.where(nv[b] > 0, j, nj - 1)

    return pl.pallas_call(
        functools.partial(_expert_body, nj=nj),
        out_shape=jax.ShapeDtypeStruct((n_rows, d), F32),
        grid_spec=pltpu.PrefetchScalarGridSpec(
            num_scalar_prefetch=3,
            grid=(n_sb, nj),
            in_specs=[
                pl.BlockSpec((None, 1, MOE_SUPER), lambda b, j, e, nv, blk: (b, 0, 0), memory_space=pltpu.SMEM),
                pl.BlockSpec((None, 1, MOE_SUPER), lambda b, j, e, nv, blk: (jnp.minimum(b + 1, n_sb - 1), 0, 0),
                             memory_space=pltpu.SMEM),
                pl.BlockSpec(memory_space=pl.ANY),
                pl.BlockSpec((None, None, d, tf), lambda b, j, e, nv, blk: (layer, e[b], 0, jj(b, j, nv))),
                pl.BlockSpec((None, None, d, tf), lambda b, j, e, nv, blk: (layer, e[b], 0, nj + jj(b, j, nv))),
                pl.BlockSpec((None, None, 1, tf), lambda b, j, e, nv, blk: (layer, e[b], 0, jj(b, j, nv))),
                pl.BlockSpec((None, None, 1, tf), lambda b, j, e, nv, blk: (layer, e[b], 0, nj + jj(b, j, nv))),
                pl.BlockSpec((None, None, tf, d), lambda b, j, e, nv, blk: (layer, e[b], jj(b, j, nv), 0)),
                pl.BlockSpec((None, None, 1, d), lambda b, j, e, nv, blk: (layer, e[b], 0, 0)),
            ],
            out_specs=pl.BlockSpec((MOE_SUPER, d), lambda b, j, e, nv, blk: (blk[b], 0)),
            scratch_shapes=[pltpu.VMEM((MOE_SUPER, d), F32), pltpu.VMEM((MOE_SUPER, d), BF16),
                            pltpu.SemaphoreType.DMA(())]),
        compiler_params=_cparams(("arbitrary", "arbitrary")),
        name="moe_experts",
    )(sb_e, sb_nv, sb_blk, row_token, row_token, h2, w_gu, w_gu, b_gu.reshape(nl, ne, 1, f2),
      b_gu.reshape(nl, ne, 1, f2), w_down, b_down.reshape(nl, ne, 1, d))


def _combine_body(dest_ref, x_ref, g2_ref, gate_ref, fg_ref, y_ref, o_ref, gbuf_ref, sem, *, final):
    nb, r, d = x_ref.shape
    n = nb * r

    def issue(i, c):
        for tt in range(GATHER_UNROLL):
            t = i * GATHER_UNROLL + tt
            for k in range(TOP_K):
                pltpu.make_async_copy(y_ref.at[pl.ds(dest_ref[0, t * TOP_K + k], 1), :],
                                      gbuf_ref.at[pl.ds(k * n + t, 1), :], sem).start()
        return c

    lax.fori_loop(0, n // GATHER_UNROLL, issue, 0)
    pltpu.make_async_copy(y_ref.at[pl.ds(0, TOP_K * n), :], gbuf_ref, sem).wait()
    gate = gate_ref[...].reshape(n, TOP_K)
    f = gbuf_ref[0:n, :] * gate[:, 0:1]
    for k in range(1, TOP_K):
        f = f + gbuf_ref[k * n:(k + 1) * n, :] * gate[:, k:k + 1]
    x2 = x_ref[...] + g2_ref[...] * f.reshape(nb, r, d)
    if final:
        x2 = x2 * lax.rsqrt(jnp.mean(x2 * x2, axis=-1, keepdims=True) + EPS) * fg_ref[...]
    o_ref[...] = x2


def _moe_combine(x1, mod, gates, dest, ybuf, final_g, final, nb, r):
    b, l, d = x1.shape
    n = nb * r
    nt = (b // nb) * (l // r)
    lt = l // r
    blk = lambda w: pl.BlockSpec((nb, r, w), lambda i, j: (i, j, 0))
    return pl.pallas_call(
        functools.partial(_combine_body, final=final),
        out_shape=jax.ShapeDtypeStruct((b, l, d), F32),
        grid=(b // nb, lt),
        in_specs=[pl.BlockSpec((None, 1, n * TOP_K), lambda i, j: (i * lt + j, 0, 0), memory_space=pltpu.SMEM),
                  blk(d),
                  pl.BlockSpec((nb, 1, d), lambda i, j: (i, 0, 5)),
                  blk(TOP_K),
                  pl.BlockSpec((1, d), lambda i, j: (0, 0)),
                  pl.BlockSpec(memory_space=pl.ANY)],
        out_specs=blk(d),
        scratch_shapes=[pltpu.VMEM((TOP_K * n, d), F32), pltpu.SemaphoreType.DMA(())],
        compiler_params=_cparams(("arbitrary", "arbitrary")),
        name="moe_combine",
    )(dest.reshape(nt, 1, n * TOP_K), x1, mod, gates, final_g, ybuf)


def _route(idx_all, n_experts):
    m = idx_all.shape[0] * TOP_K
    e_flat = idx_all.reshape(m)
    onehot = (e_flat[:, None] == jnp.arange(n_experts, dtype=jnp.int32)[None, :]).astype(jnp.int32)
    csum = jnp.cumsum(onehot, axis=0)
    counts = csum[-1]
    rank = jnp.sum(csum * onehot, axis=1) - 1
    n_sb_e = (counts + MOE_SUPER - 1) // MOE_SUPER
    sb_end = jnp.cumsum(n_sb_e)
    sb_start = sb_end - n_sb_e
    dest = sb_start[e_flat] * MOE_SUPER + rank
    n_sb = -(-m // MOE_SUPER) + n_experts + 1
    n_rows = n_sb * MOE_SUPER
    n_used = sb_end[-1]
    sb_id = jnp.arange(n_sb, dtype=jnp.int32)
    sb_clamped = jnp.minimum(sb_id, n_used - 1)
    sb_e = jnp.minimum(jnp.searchsorted(sb_end, sb_clamped, side='right'), n_experts - 1).astype(jnp.int32)
    within = sb_clamped - sb_start[sb_e]
    sb_nv = jnp.where(sb_id < n_used, jnp.clip(counts[sb_e] - within * MOE_SUPER, 0, MOE_SUPER), 0).astype(jnp.int32)
    row_token = jnp.zeros((n_rows,), jnp.int32).at[dest].set(jnp.arange(m, dtype=jnp.int32) // TOP_K)
    return (dest.astype(jnp.int32), row_token.reshape(n_sb, 1, MOE_SUPER), sb_e, sb_nv,
            sb_clamped.astype(jnp.int32))


def _prep_w_in(w_in):
    wm = W_MIX
    o = 0
    w_conv = w_in[..., o:o + 2 * wm]
    o += 2 * wm
    n_ssm = 2 * wm + 2 * SSM_GROUPS * SSM_STATE
    w_ssm = w_in[..., o:o + n_ssm]
    w_dt = w_in[..., o + n_ssm:o + n_ssm + SSM_HEADS]
    o += n_ssm + SSM_HEADS
    w_dt = jnp.pad(w_dt, ((0, 0), (0, 0), (0, DT_PAD - SSM_HEADS)))
    w_ssm = jnp.concatenate([w_ssm, w_dt], axis=-1)

    def halves_first(w):
        s = w.shape[:-1]
        return w.reshape(*s, RET_HEADS, 2, RET_HALF).swapaxes(-2, -3).reshape(*s, RET_QK)

    w_q = halves_first(w_in[..., o:o + RET_QK])
    w_k = halves_first(w_in[..., o + RET_QK:o + 2 * RET_QK])
    w_ret = jnp.concatenate([w_q, w_k, w_in[..., o + 2 * RET_QK:o + 2 * RET_QK + 2 * wm]], axis=-1)
    o += 2 * RET_QK + 2 * wm
    w_pool = w_in[..., o:o + wm]
    return [w.astype(BF16) for w in (w_conv, w_ssm, w_ret, w_pool)]


def _pad_lanes(v, width, value=0.0):
    return jnp.pad(v, ((0, 0), (0, width - v.shape[-1])), constant_values=value)


def _group_plan(b, l):
    if l >= 256:
        return (1, 256), (256, CHUNK)
    nb = max(1, min(b, 256 // l))
    return (nb, l), (l, min(CHUNK, l))


def kernel(x_prompt, x_sample, c_prompt, c_sample, state_conv, state_ssm_conv, state_ssm, state_ret, state_pool,
           w_ada, b_ada, norm1_g, norm2_g, w_in, w_out, conv_w, conv_b, conv_ln_g, conv_ln_b,
           ssm_conv_w, ssm_conv_b, ssm_dt_bias, ssm_a_log, ssm_d, ssm_norm_g, pool_w, pool_scale,
           router_w, router_b, moe_w_gu, moe_b_gu, moe_w_down, moe_b_down, final_norm_g):
    depth = w_in.shape[0]
    d = x_prompt.shape[-1]
    bp, lp, _ = x_prompt.shape
    bs, ls, _ = x_sample.shape
    n_experts = router_w.shape[-1]
    wm = W_MIX
    hp = SSM_HEADS * SSM_HEADDIM
    hk = RET_HEADS * RET_KDIM

    cb = -(-(bp + bs) // 8) * 8
    c_all = jnp.concatenate([c_prompt, c_sample, jnp.zeros((cb - bp - bs, d), F32)], axis=0)
    mod_all = _ada(c_all, w_ada, b_ada)

    w_in_parts = _prep_w_in(w_in)
    w_out_b = w_out.astype(BF16)
    pool_w_b = pool_w.astype(BF16)
    r3 = lambda a: a.reshape(depth, 1, a.shape[-1])
    n1g, n2g = r3(norm1_g), r3(norm2_g)
    conv_b3, ln_g3, ln_b3 = r3(conv_b), r3(conv_ln_g), r3(conv_ln_b)
    scb3 = r3(ssm_conv_b)
    dtb3 = r3(_pad_lanes(ssm_dt_bias, DT_PAD))
    alog3 = r3(_pad_lanes(ssm_a_log, DT_PAD))
    dsk3 = r3(jnp.repeat(ssm_d, SSM_HEADDIM, axis=-1))
    sng3 = r3(ssm_norm_g)
    psc3 = r3(pool_scale)
    rb3 = r3(router_b)
    rw_hi = router_w.astype(BF16)
    rw_lo = (router_w - rw_hi.astype(F32)).astype(BF16)
    t_all = bp * lp + bs * ls
    expander = (jnp.arange(DT_PAD)[:, None] == (jnp.arange(hp) // SSM_HEADDIM)[None, :]).astype(F32)

    groups = []
    for name, x, b, l, pos0 in (("p", x_prompt, bp, lp, 0), ("s", x_sample, bs, ls, PAST_LEN)):
        (nb, r), (rm, q) = _group_plan(b, l)
        groups.append(dict(name=name, x=x, b=b, l=l, pos0=pos0, nb=nb, r=r, rm=rm, q=q,
                           tables=_ret_tables(l, q, float(pos0)), new=([], [], [], [], [])))

    for layer in range(depth):
        mods = (mod_all[layer, :bp].reshape(bp, 1, -1), mod_all[layer, bp:bp + bs].reshape(bs, 1, -1))
        staged = []
        h2_all = None
        row0 = 0
        for gi, g in enumerate(groups):
            b, l = g["b"], g["l"]
            if g["name"] == "p":
                st_conv = jnp.zeros((b, CONV_HIST, wm), F32)
                st_sconv = jnp.zeros((b, SSD_HIST, SSM_XBC), F32)
                st_ssm = jnp.zeros((b, hp, SSM_STATE), F32)
                st_ret = jnp.zeros((b, hk, RET_VDIM), F32)
                st_pool = jnp.zeros((b, POOL_HIST, wm), F32)
            else:
                st_conv = jnp.pad(state_conv[layer], ((0, 0), (CONV_HIST - (CONV_K - 1), 0), (0, 0)))
                st_sconv = jnp.pad(state_ssm_conv[layer], ((0, 0), (SSD_HIST - (SSM_CONV_K - 1), 0), (0, 0)))
                st_ssm = state_ssm[layer].reshape(b, hp, SSM_STATE)
                st_ret = state_ret[layer].reshape(b, hk, RET_VDIM)
                st_pool = jnp.pad(state_pool[layer], ((0, 0), (POOL_HIST - POOL_BUF, 0), (0, 0)))
            mod = mods[gi]
            uc, us, ur, up = _inproj(g["x"], mod, n1g, w_in_parts, layer, g["nb"], g["r"])
            ya, n_conv = _conv_mixer(uc, st_conv, conv_w, conv_b3, ln_g3, ln_b3, layer, g["rm"])
            yb, n_ssm = _ssd_mixer(us, st_sconv, st_ssm, ssm_conv_w, scb3, dtb3, alog3, dsk3, sng3, expander,
                                   layer, g["rm"], g["q"])
            yc, n_ret = _ret_mixer(ur, st_ret, g["tables"], g["rm"], g["q"])
            yd = _pool_mixer(up, st_pool, pool_w_b, psc3, layer, g["rm"], g["pos0"])
            x1, h2_all, idx, gates = _outproj(g["x"], (ya, yb, yc, yd), mod, n2g, w_out_b, rw_hi, rw_lo, rb3,
                                              layer, g["nb"], g["r"], h2_all, t_all, row0)
            row0 += b * l
            new = g["new"]
            new[0].append(n_conv[:, CONV_HIST - (CONV_K - 1):, :])
            new[1].append(us[:, l - (SSM_CONV_K - 1):, wm:wm + SSM_XBC])
            new[2].append(n_ssm.reshape(b, SSM_HEADS, SSM_HEADDIM, SSM_STATE))
            new[3].append(n_ret.reshape(b, RET_HEADS, RET_KDIM, RET_VDIM))
            new[4].append(up[:, l - POOL_BUF:, :])
            staged.append((x1, idx, gates, mod))

        idx_all = jnp.concatenate([s[1].reshape(-1, TOP_K) for s in staged], axis=0)
        dest, row_token, sb_e, sb_nv, sb_blk = _route(idx_all, n_experts)
        ybuf = _moe_experts(h2_all, row_token, sb_e, sb_nv, sb_blk, moe_w_gu, moe_b_gu, moe_w_down, moe_b_down, layer)
        off = 0
        for g, (x1, idx, gates, mod) in zip(groups, staged):
            cnt = g["b"] * g["l"] * TOP_K
            g["x"] = _moe_combine(x1, mod, gates, dest[off:off + cnt], ybuf, final_norm_g.reshape(1, d),
                                  layer == depth - 1, g["nb"], g["r"])
            off += cnt

    outs = [g["x"] for g in groups]
    states = []
    for g in groups:
        states.extend(jnp.stack(v, axis=0) for v in g["new"])
    return (outs[0], outs[1], *states)
```

```python
import functools
import math

import numpy as np
import jax
import jax.numpy as jnp
from jax import lax
from jax.experimental import pallas as pl
from jax.experimental.pallas import tpu as pltpu

F32 = jnp.float32
BF16 = jnp.bfloat16
HI = lax.Precision.HIGHEST

EPS = 1e-6
CHUNK = 64
PAST_LEN = 1024
CONV_K = 31
SSM_CONV_K = 4
SSM_HEADS = 8
SSM_HEADDIM = 64
SSM_GROUPS = 2
SSM_STATE = 128
RET_HEADS = 4
RET_KDIM = 64
RET_VDIM = 128
ROPE_BASE = 10000.0
POOL_WINDOWS = (2, 4, 8, 16)
POOL_BUF = 15
TOP_K = 4
SWIGLU_LIMIT = 7.0
SWIGLU_ALPHA = 1.702

W_MIX = 512
LANES = 128
SUBLANES = 8
DT_PAD = LANES
VMEM_LIMIT = 56 * 1024 * 1024

MOE_SUPER = 1024
MOE_SUB = 256
MOE_TF = 256


def _cparams(sem, vmem=VMEM_LIMIT):
    return pltpu.CompilerParams(dimension_semantics=sem, vmem_limit_bytes=vmem)


def _silu(x):
    return x * jax.nn.sigmoid(x)


def _ada_body(c_ref, w_ref, b_ref, o_ref):
    c = c_ref[...]
    o_ref[...] = jnp.dot(_silu(c), w_ref[...], preferred_element_type=F32, precision=HI) + b_ref[...]


def _ada(c_all, w_ada, b_ada):
    nl, d, n6 = w_ada.shape
    cb = c_all.shape[0]
    tn = 1024
    return pl.pallas_call(
        _ada_body,
        out_shape=jax.ShapeDtypeStruct((nl, cb, n6), F32),
        grid=(nl, n6 // tn),
        in_specs=[pl.BlockSpec((cb, d), lambda l, j: (0, 0)),
                  pl.BlockSpec((None, d, tn), lambda l, j: (l, 0, j)),
                  pl.BlockSpec((None, 1, tn), lambda l, j: (l, 0, j))],
        out_specs=pl.BlockSpec((None, cb, tn), lambda l, j: (l, 0, j)),
        compiler_params=_cparams(("arbitrary", "arbitrary")),
        name="ada",
    )(c_all, w_ada, b_ada.reshape(nl, 1, n6))


def _inproj_body(x_ref, sh_ref, sc_ref, g_ref, wc_ref, ws_ref, wr_ref, wp_ref, uc_ref, us_ref, ur_ref, up_ref):
    nb, r, d = x_ref.shape
    x = x_ref[...]
    ms = jnp.mean(x * x, axis=-1, keepdims=True)
    h = x * lax.rsqrt(ms + EPS) * g_ref[...]
    h = h * (1.0 + sc_ref[...]) + sh_ref[...]
    hb = h.reshape(nb * r, d).astype(BF16)
    for w_ref, u_ref in ((wc_ref, uc_ref), (ws_ref, us_ref), (wr_ref, ur_ref), (wp_ref, up_ref)):
        u = jnp.dot(hb, w_ref[...], preferred_element_type=F32)
        u_ref[...] = u.reshape(nb, r, u.shape[-1])


def _inproj(x, mod, norm_g, ws, layer, nb, r):
    b, l, d = x.shape
    grid = (b // nb, l // r)
    xs = pl.BlockSpec((nb, r, d), lambda i, j: (i, j, 0))
    wspecs = [pl.BlockSpec((None, d, w.shape[-1]), lambda i, j: (layer, 0, 0), pipeline_mode=pl.Buffered(1)) for w in ws]
    return pl.pallas_call(
        _inproj_body,
        out_shape=[jax.ShapeDtypeStruct((b, l, w.shape[-1]), F32) for w in ws],
        grid=grid,
        in_specs=[xs,
                  pl.BlockSpec((nb, 1, d), lambda i, j: (i, 0, 0)),
                  pl.BlockSpec((nb, 1, d), lambda i, j: (i, 0, 1)),
                  pl.BlockSpec((None, 1, d), lambda i, j: (layer, 0, 0))] + wspecs,
        out_specs=[pl.BlockSpec((nb, r, w.shape[-1]), lambda i, j: (i, j, 0)) for w in ws],
        compiler_params=_cparams(("arbitrary", "arbitrary")),
        name="inproj",
    )(x, mod, mod, norm_g, *ws)


CONV_HIST = 32
CONV_SUB = 64


def _conv_body(u_ref, st_ref, w_ref, b_ref, lg_ref, lb_ref, y_ref, ns_ref, hist_ref, win_ref):
    r = u_ref.shape[0]
    wm = W_MIX

    @pl.when(pl.program_id(1) == 0)
    def _():
        hist_ref[0:CONV_HIST, :] = st_ref[...]

    u = u_ref[...]
    hist_ref[CONV_HIST:CONV_HIST + r, :] = u[:, :wm] * jax.nn.sigmoid(u[:, wm:])
    sub = min(CONV_SUB, r)
    for r0 in range(0, r, sub):
        acc = jnp.zeros((sub, wm), F32)
        for s in range(SUBLANES):
            a_max = (CONV_K - 1 - s) // SUBLANES
            lo = CONV_HIST + r0 - s - SUBLANES * a_max
            rows = sub + SUBLANES * a_max
            win_ref[0:rows, :] = hist_ref[lo:lo + rows, :]
            for a in range(a_max + 1):
                j = s + SUBLANES * a
                off = SUBLANES * (a_max - a)
                acc = acc + w_ref[CONV_K - 1 - j:CONV_K - j, :] * win_ref[off:off + sub, :]
        acc = acc + b_ref[...]
        mu = jnp.mean(acc, axis=-1, keepdims=True)
        xc = acc - mu
        var = jnp.mean(xc * xc, axis=-1, keepdims=True)
        yn = xc * lax.rsqrt(var + EPS) * lg_ref[...] + lb_ref[...]
        y_ref[r0:r0 + sub, :] = _silu(yn).astype(y_ref.dtype)
    last = hist_ref[r:r + CONV_HIST, :]
    ns_ref[...] = last
    hist_ref[0:CONV_HIST, :] = last


def _conv_mixer(u, st, conv_w, conv_b, ln_g, ln_b, layer, r):
    b, l, _ = u.shape
    wm = W_MIX
    par = lambda k: pl.BlockSpec((None, k, wm), lambda i, j: (layer, 0, 0))
    return pl.pallas_call(
        _conv_body,
        out_shape=[jax.ShapeDtypeStruct((b, l, wm), BF16), jax.ShapeDtypeStruct((b, CONV_HIST, wm), F32)],
        grid=(b, l // r),
        in_specs=[pl.BlockSpec((None, r, 2 * wm), lambda i, j: (i, j, 0)),
                  pl.BlockSpec((None, CONV_HIST, wm), lambda i, j: (i, 0, 0)),
                  par(CONV_K), par(1), par(1), par(1)],
        out_specs=[pl.BlockSpec((None, r, wm), lambda i, j: (i, j, 0)),
                   pl.BlockSpec((None, CONV_HIST, wm), lambda i, j: (i, 0, 0))],
        scratch_shapes=[pltpu.VMEM((r + CONV_HIST, wm), F32),
                        pltpu.VMEM((min(CONV_SUB, r) + CONV_HIST, wm), F32)],
        compiler_params=_cparams(("arbitrary", "arbitrary")),
        name="conv_mixer",
    )(u, st, conv_w, conv_b, ln_g, ln_b)


POOL_HIST = 16


def _pool_body(p_ref, st_ref, w_ref, sc_ref, y_ref, hist_ref, *, pos0):
    r = p_ref.shape[0]
    i = pl.program_id(1)

    @pl.when(i == 0)
    def _():
        hist_ref[0:POOL_HIST, :] = st_ref[...]

    p = p_ref[...]
    hist_ref[POOL_HIST:POOL_HIST + r, :] = p
    pos = pos0 + i * r + lax.broadcasted_iota(jnp.int32, (r, 1), 0)
    outs = []
    for gi, w in enumerate(POOL_WINDOWS):
        lo = gi * LANES
        s = p[:, lo:lo + LANES]
        for j in range(1, w):
            s = s + hist_ref[POOL_HIST - j:POOL_HIST - j + r, lo:lo + LANES]
        cnt = jnp.minimum(pos + 1, w).astype(F32)
        pooled = s / cnt - p[:, lo:lo + LANES]
        outs.append(jnp.dot(pooled.astype(BF16), w_ref[gi], preferred_element_type=F32))
    y = jnp.concatenate(outs, axis=-1) * sc_ref[...]
    y_ref[...] = y.astype(y_ref.dtype)
    hist_ref[0:POOL_HIST, :] = hist_ref[r:r + POOL_HIST, :]


def _pool_mixer(p, st, pool_w, pool_scale, layer, r, pos0):
    b, l, wm = p.shape
    return pl.pallas_call(
        functools.partial(_pool_body, pos0=pos0),
        out_shape=jax.ShapeDtypeStruct((b, l, wm), BF16),
        grid=(b, l // r),
        in_specs=[pl.BlockSpec((None, r, wm), lambda i, j: (i, j, 0)),
                  pl.BlockSpec((None, POOL_HIST, wm), lambda i, j: (i, 0, 0)),
                  pl.BlockSpec((None, len(POOL_WINDOWS), LANES, LANES), lambda i, j: (layer, 0, 0, 0)),
                  pl.BlockSpec((None, 1, wm), lambda i, j: (layer, 0, 0))],
        out_specs=pl.BlockSpec((None, r, wm), lambda i, j: (i, j, 0)),
        scratch_shapes=[pltpu.VMEM((r + POOL_HIST, wm), F32)],
        compiler_params=_cparams(("arbitrary", "arbitrary")),
        name="pool_mixer",
    )(p, st, pool_w, pool_scale)


SSD_HIST = 8
SSM_XBC = W_MIX + 2 * SSM_GROUPS * SSM_STATE
GROUP_W = W_MIX // SSM_GROUPS


def _ssd_body(u_ref, cst_ref, hst_ref, cw_ref, cb_ref, dtb_ref, alog_ref, dsk_ref, ng_ref, ex_ref,
              y_ref, nh_ref, hist_ref, st_ref, *, q):
    r = u_ref.shape[0]
    wm = W_MIX
    n = SSM_STATE
    i = pl.program_id(1)

    @pl.when(i == 0)
    def _():
        hist_ref[0:SSD_HIST, :] = cst_ref[...]
        st_ref[...] = hst_ref[...].T

    hist_ref[SSD_HIST:SSD_HIST + r, :] = u_ref[:, wm:wm + SSM_XBC]
    a_neg = -jnp.exp(alog_ref[...])
    ex = ex_ref[...]
    row = lax.broadcasted_iota(jnp.int32, (q, q), 0)
    col = lax.broadcasted_iota(jnp.int32, (q, q), 1)
    tril = row >= col
    ltri = tril.astype(F32)
    eye = (lax.broadcasted_iota(jnp.int32, (LANES, LANES), 0)
           == lax.broadcasted_iota(jnp.int32, (LANES, LANES), 1)).astype(F32)
    head_of_lane = lax.broadcasted_iota(jnp.int32, (q, GROUP_W), 1) // SSM_HEADDIM
    heads_per_group = SSM_HEADS // SSM_GROUPS

    for c in range(r // q):
        r0 = c * q
        xc = jnp.zeros((q, SSM_XBC), F32)
        for k in range(SSM_CONV_K):
            lo = SSD_HIST + r0 + k - (SSM_CONV_K - 1)
            xc = xc + cw_ref[k:k + 1, :] * hist_ref[lo:lo + q, :]
        xc = _silu(xc + cb_ref[...])
        xs = xc[:, :wm]
        bm = xc[:, wm:wm + SSM_GROUPS * n]
        cm = xc[:, wm + SSM_GROUPS * n:]
        z = u_ref[r0:r0 + q, 0:wm]
        dt = jax.nn.softplus(u_ref[r0:r0 + q, wm + SSM_XBC:wm + SSM_XBC + DT_PAD] + dtb_ref[...])
        cum = jnp.dot(ltri, dt * a_neg, preferred_element_type=F32, precision=HI)
        cum_t = lax.dot_general(eye, cum, (((1,), (1,)), ((), ())), preferred_element_type=F32, precision=HI)
        dt_full = jnp.dot(dt, ex, preferred_element_type=F32, precision=HI)
        cum_full = jnp.dot(cum, ex, preferred_element_type=F32, precision=HI)
        cum_last = cum_full[q - 1:q, :]
        xdt = xs * dt_full
        xw = xdt * jnp.exp(cum_last - cum_full)
        ecum = jnp.exp(cum_full)
        ys = []
        for g in range(SSM_GROUPS):
            gl = slice(g * GROUP_W, (g + 1) * GROUP_W)
            bg = bm[:, g * n:(g + 1) * n].astype(BF16)
            cg = cm[:, g * n:(g + 1) * n].astype(BF16)
            cb = lax.dot_general(cg, bg, (((1,), (1,)), ((), ())), preferred_element_type=F32)
            xdt_g = xdt[:, gl].astype(BF16)
            yg = jnp.zeros((q, GROUP_W), F32)
            for hh in range(heads_per_group):
                h = g * heads_per_group + hh
                seg = cum_full[:, h * SSM_HEADDIM:h * SSM_HEADDIM + 1] - cum_t[h:h + 1, :]
                dec = jnp.exp(jnp.where(tril, seg, -jnp.inf))
                ph = jnp.dot((cb * dec).astype(BF16), xdt_g, preferred_element_type=F32)
                yg = jnp.where(head_of_lane == hh, ph, yg)
            st_g = st_ref[:, gl]
            yg = yg + ecum[:, gl] * jnp.dot(cg, st_g.astype(BF16), preferred_element_type=F32)
            upd = lax.dot_general(bg, xw[:, gl].astype(BF16), (((0,), (0,)), ((), ())), preferred_element_type=F32)
            st_ref[:, gl] = st_g * jnp.exp(cum_last[:, gl]) + upd
            ys.append(yg)
        y = jnp.concatenate(ys, axis=-1) + xs * dsk_ref[...]
        y = y * _silu(z)
        y = y * lax.rsqrt(jnp.mean(y * y, axis=-1, keepdims=True) + EPS) * ng_ref[...]
        y_ref[r0:r0 + q, :] = y.astype(y_ref.dtype)

    hist_ref[0:SSD_HIST, :] = hist_ref[r:r + SSD_HIST, :]

    @pl.when(i == pl.num_programs(1) - 1)
    def _():
        nh_ref[...] = st_ref[...].T


def _ssd_mixer(u, cst, hst, cw, cb, dtb, alog, dsk, ng, ex, layer, r, q):
    b, l, wu = u.shape
    wm = W_MIX
    hp = SSM_HEADS * SSM_HEADDIM
    par = lambda k, w: pl.BlockSpec((None, k, w), lambda i, j: (layer, 0, 0))
    return pl.pallas_call(
        functools.partial(_ssd_body, q=q),
        out_shape=[jax.ShapeDtypeStruct((b, l, wm), BF16), jax.ShapeDtypeStruct((b, hp, SSM_STATE), F32)],
        grid=(b, l // r),
        in_specs=[pl.BlockSpec((None, r, wu), lambda i, j: (i, j, 0)),
                  pl.BlockSpec((None, SSD_HIST, SSM_XBC), lambda i, j: (i, 0, 0)),
                  pl.BlockSpec((None, hp, SSM_STATE), lambda i, j: (i, 0, 0)),
                  par(SSM_CONV_K, SSM_XBC), par(1, SSM_XBC), par(1, DT_PAD), par(1, DT_PAD), par(1, wm), par(1, wm),
                  pl.BlockSpec((DT_PAD, wm), lambda i, j: (0, 0))],
        out_specs=[pl.BlockSpec((None, r, wm), lambda i, j: (i, j, 0)),
                   pl.BlockSpec((None, hp, SSM_STATE), lambda i, j: (i, 0, 0))],
        scratch_shapes=[pltpu.VMEM((r + SSD_HIST, SSM_XBC), F32), pltpu.VMEM((SSM_STATE, hp), F32)],
        compiler_params=_cparams(("arbitrary", "arbitrary")),
        name="ssd_mixer",
    )(u, cst, hst, cw, cb, dtb, alog, dsk, ng, ex)


RET_QK = RET_HEADS * RET_KDIM
RET_HALF = RET_KDIM // 2


def _ret_body(u_ref, st_ref, cos_ref, sin_ref, intra_ref, kdec_ref, qdec_ref, cdec_ref, y_ref, ns_ref, s_ref, *, q):
    r = u_ref.shape[0]
    wm = W_MIX
    i = pl.program_id(1)
    dv = RET_VDIM

    @pl.when(i == 0)
    def _():
        s_ref[...] = jnp.zeros_like(s_ref)
        for h in range(RET_HEADS):
            for b in range(2):
                s_ref[LANES * b + RET_HALF * h:LANES * b + RET_HALF * (h + 1), dv * h:dv * (h + 1)] = (
                    st_ref[RET_KDIM * h + RET_HALF * b:RET_KDIM * h + RET_HALF * (b + 1), :])

    head_of_qlane = (lax.broadcasted_iota(jnp.int32, (q, RET_QK), 1) % LANES) // RET_HALF
    bd_mask = ((lax.broadcasted_iota(jnp.int32, (RET_QK, wm), 0) % LANES) // RET_HALF
               == lax.broadcasted_iota(jnp.int32, (RET_QK, wm), 1) // dv)

    for c in range(r // q):
        r0 = c * q
        cos = cos_ref[r0:r0 + q, :]
        sin = sin_ref[r0:r0 + q, :]
        q1 = u_ref[r0:r0 + q, 0:LANES]
        q2 = u_ref[r0:r0 + q, LANES:2 * LANES]
        k1 = u_ref[r0:r0 + q, RET_QK:RET_QK + LANES]
        k2 = u_ref[r0:r0 + q, RET_QK + LANES:2 * RET_QK]
        v = u_ref[r0:r0 + q, 2 * RET_QK:2 * RET_QK + wm]
        gate = u_ref[r0:r0 + q, 2 * RET_QK + wm:2 * RET_QK + 2 * wm]
        qr = jnp.concatenate([q1 * cos - q2 * sin, q1 * sin + q2 * cos], axis=-1)
        kr = jnp.concatenate([k1 * cos - k2 * sin, k1 * sin + k2 * cos], axis=-1) * (RET_KDIM ** -0.5)
        qrb = qr.astype(BF16)
        krb = kr.astype(BF16)
        vb = v.astype(BF16)
        outs = []
        for h in range(RET_HEADS):
            qm = jnp.where(head_of_qlane == h, qr, 0.0).astype(BF16)
            sc = lax.dot_general(qm, krb, (((1,), (1,)), ((), ())), preferred_element_type=F32)
            ph = (sc * intra_ref[h]).astype(BF16)
            outs.append(jnp.dot(ph, vb[:, dv * h:dv * (h + 1)], preferred_element_type=F32))
        o = jnp.concatenate(outs, axis=-1)
        s_old = s_ref[...]
        o = o + qdec_ref[...] * jnp.dot(qrb, s_old.astype(BF16), preferred_element_type=F32)
        kv = lax.dot_general(krb, (v * kdec_ref[...]).astype(BF16), (((0,), (0,)), ((), ())), preferred_element_type=F32)
        s_ref[...] = s_old * cdec_ref[...] + jnp.where(bd_mask, kv, 0.0)
        normed = []
        for h in range(RET_HEADS):
            oh = o[:, dv * h:dv * (h + 1)]
            normed.append(oh * lax.rsqrt(jnp.mean(oh * oh, axis=-1, keepdims=True) + EPS))
        o = jnp.concatenate(normed, axis=-1) * _silu(gate)
        y_ref[r0:r0 + q, :] = o.astype(y_ref.dtype)

    @pl.when(i == pl.num_programs(1) - 1)
    def _():
        for h in range(RET_HEADS):
            for b in range(2):
                ns_ref[RET_KDIM * h + RET_HALF * b:RET_KDIM * h + RET_HALF * (b + 1), :] = (
                    s_ref[LANES * b + RET_HALF * h:LANES * b + RET_HALF * (h + 1), dv * h:dv * (h + 1)])


def _ret_tables(l, q, pos0):
    inv = 1.0 / (ROPE_BASE ** jnp.linspace(0.0, 1.0, RET_HALF, dtype=F32))
    pos = pos0 + jnp.arange(l, dtype=F32)
    ang = pos[:, None] * inv[None, :]
    cos = jnp.tile(jnp.cos(ang), (1, RET_HEADS))
    sin = jnp.tile(jnp.sin(ang), (1, RET_HEADS))
    log_gamma = jnp.log(1.0 - 2.0 ** (-5.0 - jnp.arange(RET_HEADS, dtype=F32)))
    idx = jnp.arange(q, dtype=F32)
    intra = jnp.exp(jnp.abs(idx[:, None] - idx[None, :])[None, :, :] * log_gamma[:, None, None])
    kdec = jnp.repeat(jnp.exp((q - idx)[:, None] * log_gamma), RET_VDIM, axis=1)
    qdec = jnp.repeat(jnp.exp(idx[:, None] * log_gamma), RET_VDIM, axis=1)
    cdec = jnp.repeat(jnp.exp(q * log_gamma), RET_VDIM)[None, :]
    return cos, sin, intra, kdec, qdec, cdec


def _ret_mixer(u, st, tables, r, q):
    b, l, wu = u.shape
    wm = W_MIX
    cos, sin, intra, kdec, qdec, cdec = tables
    hk = RET_HEADS * RET_KDIM
    full = lambda a: pl.BlockSpec(a.shape, lambda i, j: (0,) * a.ndim)
    return pl.pallas_call(
        functools.partial(_ret_body, q=q),
        out_shape=[jax.ShapeDtypeStruct((b, l, wm), BF16), jax.ShapeDtypeStruct((b, hk, RET_VDIM), F32)],
        grid=(b, l // r),
        in_specs=[pl.BlockSpec((None, r, wu), lambda i, j: (i, j, 0)),
                  pl.BlockSpec((None, hk, RET_VDIM), lambda i, j: (i, 0, 0)),
                  pl.BlockSpec((r, LANES), lambda i, j: (j, 0)),
                  pl.BlockSpec((r, LANES), lambda i, j: (j, 0)),
                  full(intra), full(kdec), full(qdec), full(cdec)],
        out_specs=[pl.BlockSpec((None, r, wm), lambda i, j: (i, j, 0)),
                   pl.BlockSpec((None, hk, RET_VDIM), lambda i, j: (i, 0, 0))],
        scratch_shapes=[pltpu.VMEM((RET_QK, wm), F32)],
        compiler_params=_cparams(("arbitrary", "arbitrary")),
        name="ret_mixer",
    )(u, st, cos, sin, intra, kdec, qdec, cdec)


def _outproj_body(x_ref, ya_ref, yb_ref, yc_ref, yd_ref, g1_ref, sh_ref, sc_ref, ng_ref, wo_ref, rwh_ref, rwl_ref,
                  rb_ref, *rest):
    x1_ref, h2_ref, idx_ref, gate_ref = rest[-4:]
    nb, r, d = x_ref.shape
    n = nb * r
    ycat = jnp.concatenate([ya_ref[...], yb_ref[...], yc_ref[...], yd_ref[...]], axis=-1).reshape(n, d)
    y = jnp.dot(ycat, wo_ref[...], preferred_element_type=F32).reshape(nb, r, d)
    x1 = x_ref[...] + g1_ref[...] * y
    x1_ref[...] = x1
    ms = jnp.mean(x1 * x1, axis=-1, keepdims=True)
    h2 = x1 * lax.rsqrt(ms + EPS) * ng_ref[...]
    h2 = (h2 * (1.0 + sc_ref[...]) + sh_ref[...]).reshape(n, d)
    h2_ref[...] = h2
    h2h = h2.astype(BF16)
    h2l = (h2 - h2h.astype(F32)).astype(BF16)
    rwh = rwh_ref[...]
    logits = (jnp.dot(h2h, rwh, preferred_element_type=F32) + jnp.dot(h2l, rwh, preferred_element_type=F32)
              + jnp.dot(h2h, rwl_ref[...], preferred_element_type=F32) + rb_ref[...])
    ne = logits.shape[-1]
    lane = lax.broadcasted_iota(jnp.int32, (n, ne), 1)
    vals, idxs = [], []
    for _ in range(TOP_K):
        m = jnp.max(logits, axis=-1, keepdims=True)
        idx = jnp.min(jnp.where(logits == m, lane, ne), axis=-1, keepdims=True)
        vals.append(m)
        idxs.append(idx)
        logits = jnp.where(lane == idx, -jnp.inf, logits)
    v = jnp.concatenate(vals, axis=-1)
    ex = jnp.exp(v - v[:, 0:1])
    gate_ref[...] = (ex / jnp.sum(ex, axis=-1, keepdims=True)).reshape(nb, r, TOP_K)
    idx_ref[...] = jnp.concatenate(idxs, axis=-1).reshape(nb, r, TOP_K)


def _outproj(x, ys, mod, norm_g, w_out, rw_hi, rw_lo, router_b, layer, nb, r, h2_shared, t_all, row0):
    b, l, d = x.shape
    ne = rw_hi.shape[-1]
    n = nb * r
    lt = l // r
    blk0 = row0 // n
    blk = lambda w: pl.BlockSpec((nb, r, w), lambda i, j: (i, j, 0))
    modspec = lambda k: pl.BlockSpec((nb, 1, d), lambda i, j: (i, 0, k))
    in_specs = [blk(d), blk(W_MIX), blk(W_MIX), blk(W_MIX), blk(W_MIX),
                modspec(2), modspec(3), modspec(4),
                pl.BlockSpec((None, 1, d), lambda i, j: (layer, 0, 0)),
                pl.BlockSpec((None, d, d), lambda i, j: (layer, 0, 0), pipeline_mode=pl.Buffered(1)),
                pl.BlockSpec((None, d, ne), lambda i, j: (layer, 0, 0)),
                pl.BlockSpec((None, d, ne), lambda i, j: (layer, 0, 0)),
                pl.BlockSpec((None, 1, ne), lambda i, j: (layer, 0, 0))]
    args = [x, *ys, mod, mod, mod, norm_g, w_out, rw_hi, rw_lo, router_b]
    aliases = {}
    if h2_shared is not None:
        in_specs.append(pl.BlockSpec(memory_space=pl.ANY))
        args.append(h2_shared)
        aliases = {len(args) - 1: 1}
    return pl.pallas_call(
        _outproj_body,
        out_shape=[jax.ShapeDtypeStruct((b, l, d), F32), jax.ShapeDtypeStruct((t_all, d), F32),
                   jax.ShapeDtypeStruct((b, l, TOP_K), jnp.int32), jax.ShapeDtypeStruct((b, l, TOP_K), F32)],
        grid=(b // nb, lt),
        in_specs=in_specs,
        out_specs=[blk(d), pl.BlockSpec((n, d), lambda i, j: (blk0 + i * lt + j, 0)), blk(TOP_K), blk(TOP_K)],
        input_output_aliases=aliases,
        compiler_params=_cparams(("arbitrary", "arbitrary")),
        name="outproj",
    )(*args)


MOE_SUBS = MOE_SUPER // MOE_SUB
MOE_DOWN_COLS = 512
GATHER_UNROLL = 8
MOE_DMA_PLAN = (
    (True, (2, 0), (2, 0)),
    (True, (1, 1), (1, 1)),
    (False, (2, 0), (2, 0)),
    (False, (1, 0), (1, 0)),
)


def _expert_body(e_ref, nv_ref, blk_ref, tok_ref, tokn_ref, h_ref, wg_ref, wu_ref, bg_ref, bu_ref, wd_ref, bd_ref,
                 o_ref, xg_ref, xb_ref, sem, *, nj, spread, prio):
    b = pl.program_id(0)
    j = pl.program_id(1)
    nv = nv_ref[b]
    nsub = (nv + MOE_SUB - 1) // MOE_SUB
    per_step = MOE_SUPER // nj
    n_phase = 2 + o_ref.shape[-1] // MOE_DOWN_COLS
    per_phase = per_step // n_phase + (1 if per_step % n_phase else 0)

    def row_copy(tref, r):
        return pltpu.make_async_copy(h_ref.at[pl.ds(tref[0, r], 1), :], xg_ref.at[pl.ds(r, 1), :], sem)

    def start(tref, r, k):
        row_copy(tref, r).start(priority=prio[1] + k % prio[0])

    def issue_share(phase):
        if spread:
            base = j * per_step
            for k in range(phase * per_phase, min((phase + 1) * per_phase, per_step)):
                start(tokn_ref, base + k, k)

    def issue_rows(tref, n_rows):
        def body(i, c):
            for k in range(GATHER_UNROLL):
                start(tref, i * GATHER_UNROLL + k, k)
            return c
        lax.fori_loop(0, n_rows // GATHER_UNROLL, body, 0)

    def wait_rows(n_rows):
        pltpu.make_async_copy(h_ref.at[pl.ds(0, n_rows), :], xg_ref.at[pl.ds(0, n_rows), :], sem).wait()

    @pl.when(j == 0)
    def _():
        @pl.when(b == 0)
        def _():
            issue_rows(tok_ref, MOE_SUPER if spread else nsub * MOE_SUB)

        if spread:
            @pl.when(jnp.logical_or(b == 0, nv_ref[jnp.maximum(b - 1, 0)] > 0))
            def _():
                wait_rows(MOE_SUPER)
        else:
            for n in range(1, MOE_SUBS + 1):
                @pl.when(nsub == n)
                def _():
                    wait_rows(n * MOE_SUB)

        for s in range(MOE_SUBS):
            rows = slice(s * MOE_SUB, (s + 1) * MOE_SUB)

            @pl.when(s < nsub)
            def _():
                xb_ref[rows, :] = xg_ref[rows, :].astype(BF16)

        if not spread:
            nvn = nv_ref[jnp.minimum(b + 1, pl.num_programs(0) - 1)]

            @pl.when(b + 1 < pl.num_programs(0))
            def _():
                issue_rows(tokn_ref, (nvn + MOE_SUB - 1) // MOE_SUB * MOE_SUB)

        @pl.when(nv > 0)
        def _():
            o_ref[...] = jnp.broadcast_to(bd_ref[...], o_ref.shape)

    for n in range(1, MOE_SUBS + 1):
        m = n * MOE_SUB

        @pl.when(nsub == n)
        def _():
            x = xb_ref[0:m, :]
            issue_share(0)
            g = jnp.dot(x, wg_ref[...].astype(BF16), preferred_element_type=F32) + bg_ref[...]
            issue_share(1)
            u = jnp.dot(x, wu_ref[...].astype(BF16), preferred_element_type=F32) + bu_ref[...]
            g = jnp.minimum(g, SWIGLU_LIMIT)
            u = jnp.clip(u, -SWIGLU_LIMIT, SWIGLU_LIMIT)
            act = ((u + 1.0) * (g * jax.nn.sigmoid(SWIGLU_ALPHA * g))).astype(BF16)
            for ci, c0 in enumerate(range(0, o_ref.shape[-1], MOE_DOWN_COLS)):
                cols = slice(c0, c0 + MOE_DOWN_COLS)
                issue_share(2 + ci)
                o_ref[0:m, cols] += jnp.dot(act, wd_ref[:, cols].astype(BF16), preferred_element_type=F32)


def _moe_experts(h2, row_token, sb_e, sb_nv, sb_blk, w_gu, b_gu, w_down, b_down, layer, spread, prio):
    t, d = h2.shape
    n_sb = row_token.shape[0]
    n_rows = n_sb * MOE_SUPER
    nl, ne, _, f2 = w_gu.shape
    f = f2 // 2
    tf = min(MOE_TF, f)
    nj = f // tf

    def jj(b, j, nv):
        return jnp.where(nv[b] > 0, j, nj - 1)

    return pl.pallas_call(
        functools.partial(_expert_body, nj=nj, spread=spread, prio=prio),
        out_shape=jax.ShapeDtypeStruct((n_rows, d), F32),
        grid_spec=pltpu.PrefetchScalarGridSpec(
            num_scalar_prefetch=3,
            grid=(n_sb, nj),
            in_specs=[
                pl.BlockSpec((None, 1, MOE_SUPER), lambda b, j, e, nv, blk: (b, 0, 0), memory_space=pltpu.SMEM),
                pl.BlockSpec((None, 1, MOE_SUPER), lambda b, j, e, nv, blk: (jnp.minimum(b + 1, n_sb - 1), 0, 0),
                             memory_space=pltpu.SMEM),
                pl.BlockSpec(memory_space=pl.ANY),
                pl.BlockSpec((None, None, d, tf), lambda b, j, e, nv, blk: (layer, e[b], 0, jj(b, j, nv))),
                pl.BlockSpec((None, None, d, tf), lambda b, j, e, nv, blk: (layer, e[b], 0, nj + jj(b, j, nv))),
                pl.BlockSpec((None, None, 1, tf), lambda b, j, e, nv, blk: (layer, e[b], 0, jj(b, j, nv))),
                pl.BlockSpec((None, None, 1, tf), lambda b, j, e, nv, blk: (layer, e[b], 0, nj + jj(b, j, nv))),
                pl.BlockSpec((None, None, tf, d), lambda b, j, e, nv, blk: (layer, e[b], jj(b, j, nv), 0)),
                pl.BlockSpec((None, None, 1, d), lambda b, j, e, nv, blk: (layer, e[b], 0, 0)),
            ],
            out_specs=pl.BlockSpec((MOE_SUPER, d), lambda b, j, e, nv, blk: (blk[b], 0)),
            scratch_shapes=[pltpu.VMEM((MOE_SUPER, d), F32), pltpu.VMEM((MOE_SUPER, d), BF16),
                            pltpu.SemaphoreType.DMA(())]),
        compiler_params=_cparams(("arbitrary", "arbitrary")),
        name="moe_experts",
    )(sb_e, sb_nv, sb_blk, row_token, row_token, h2, w_gu, w_gu, b_gu.reshape(nl, ne, 1, f2),
      b_gu.reshape(nl, ne, 1, f2), w_down, b_down.reshape(nl, ne, 1, d))


def _combine_body(dest_ref, x_ref, g2_ref, gate_ref, fg_ref, y_ref, o_ref, gbuf_ref, sem, *, final, prio):
    nb, r, d = x_ref.shape
    n = nb * r

    def issue(i, c):
        for tt in range(GATHER_UNROLL):
            t = i * GATHER_UNROLL + tt
            for k in range(TOP_K):
                pltpu.make_async_copy(y_ref.at[pl.ds(dest_ref[0, t * TOP_K + k], 1), :],
                                      gbuf_ref.at[pl.ds(k * n + t, 1), :], sem).start(priority=prio[1] + k % prio[0])
        return c

    lax.fori_loop(0, n // GATHER_UNROLL, issue, 0)
    pltpu.make_async_copy(y_ref.at[pl.ds(0, TOP_K * n), :], gbuf_ref, sem).wait()
    gate = gate_ref[...].reshape(n, TOP_K)
    f = gbuf_ref[0:n, :] * gate[:, 0:1]
    for k in range(1, TOP_K):
        f = f + gbuf_ref[k * n:(k + 1) * n, :] * gate[:, k:k + 1]
    x2 = x_ref[...] + g2_ref[...] * f.reshape(nb, r, d)
    if final:
        x2 = x2 * lax.rsqrt(jnp.mean(x2 * x2, axis=-1, keepdims=True) + EPS) * fg_ref[...]
    o_ref[...] = x2


def _moe_combine(x1, mod, gates, dest, ybuf, final_g, final, prio, nb, r):
    b, l, d = x1.shape
    n = nb * r
    nt = (b // nb) * (l // r)
    lt = l // r
    blk = lambda w: pl.BlockSpec((nb, r, w), lambda i, j: (i, j, 0))
    return pl.pallas_call(
        functools.partial(_combine_body, final=final, prio=prio),
        out_shape=jax.ShapeDtypeStruct((b, l, d), F32),
        grid=(b // nb, lt),
        in_specs=[pl.BlockSpec((None, 1, n * TOP_K), lambda i, j: (i * lt + j, 0, 0), memory_space=pltpu.SMEM),
                  blk(d),
                  pl.BlockSpec((nb, 1, d), lambda i, j: (i, 0, 5)),
                  blk(TOP_K),
                  pl.BlockSpec((1, d), lambda i, j: (0, 0)),
                  pl.BlockSpec(memory_space=pl.ANY)],
        out_specs=blk(d),
        scratch_shapes=[pltpu.VMEM((TOP_K * n, d), F32), pltpu.SemaphoreType.DMA(())],
        compiler_params=_cparams(("arbitrary", "arbitrary")),
        name="moe_combine",
    )(dest.reshape(nt, 1, n * TOP_K), x1, mod, gates, final_g, ybuf)


def _route(idx_all, n_experts):
    m = idx_all.shape[0] * TOP_K
    e_flat = idx_all.reshape(m)
    onehot = (e_flat[:, None] == jnp.arange(n_experts, dtype=jnp.int32)[None, :]).astype(jnp.int32)
    csum = jnp.cumsum(onehot, axis=0)
    counts = csum[-1]
    rank = jnp.sum(csum * onehot, axis=1) - 1
    n_sb_e = (counts + MOE_SUPER - 1) // MOE_SUPER
    sb_end = jnp.cumsum(n_sb_e)
    sb_start = sb_end - n_sb_e
    dest = sb_start[e_flat] * MOE_SUPER + rank
    n_sb = -(-m // MOE_SUPER) + n_experts + 1
    n_rows = n_sb * MOE_SUPER
    n_used = sb_end[-1]
    sb_id = jnp.arange(n_sb, dtype=jnp.int32)
    sb_clamped = jnp.minimum(sb_id, n_used - 1)
    sb_e = jnp.minimum(jnp.searchsorted(sb_end, sb_clamped, side='right'), n_experts - 1).astype(jnp.int32)
    within = sb_clamped - sb_start[sb_e]
    sb_nv = jnp.where(sb_id < n_used, jnp.clip(counts[sb_e] - within * MOE_SUPER, 0, MOE_SUPER), 0).astype(jnp.int32)
    row_token = jnp.zeros((n_rows,), jnp.int32).at[dest].set(jnp.arange(m, dtype=jnp.int32) // TOP_K)
    return (dest.astype(jnp.int32), row_token.reshape(n_sb, 1, MOE_SUPER), sb_e, sb_nv,
            sb_clamped.astype(jnp.int32))


def _prep_w_in(w_in):
    wm = W_MIX
    o = 0
    w_conv = w_in[..., o:o + 2 * wm]
    o += 2 * wm
    n_ssm = 2 * wm + 2 * SSM_GROUPS * SSM_STATE
    w_ssm = w_in[..., o:o + n_ssm]
    w_dt = w_in[..., o + n_ssm:o + n_ssm + SSM_HEADS]
    o += n_ssm + SSM_HEADS
    w_dt = jnp.pad(w_dt, ((0, 0), (0, 0), (0, DT_PAD - SSM_HEADS)))
    w_ssm = jnp.concatenate([w_ssm, w_dt], axis=-1)

    def halves_first(w):
        s = w.shape[:-1]
        return w.reshape(*s, RET_HEADS, 2, RET_HALF).swapaxes(-2, -3).reshape(*s, RET_QK)

    w_q = halves_first(w_in[..., o:o + RET_QK])
    w_k = halves_first(w_in[..., o + RET_QK:o + 2 * RET_QK])
    w_ret = jnp.concatenate([w_q, w_k, w_in[..., o + 2 * RET_QK:o + 2 * RET_QK + 2 * wm]], axis=-1)
    o += 2 * RET_QK + 2 * wm
    w_pool = w_in[..., o:o + wm]
    return [w.astype(BF16) for w in (w_conv, w_ssm, w_ret, w_pool)]


def _pad_lanes(v, width, value=0.0):
    return jnp.pad(v, ((0, 0), (0, width - v.shape[-1])), constant_values=value)


def _group_plan(b, l):
    if l >= 256:
        return (1, 256), (256, CHUNK)
    nb = max(1, min(b, 256 // l))
    return (nb, l), (l, min(CHUNK, l))


def kernel(x_prompt, x_sample, c_prompt, c_sample, state_conv, state_ssm_conv, state_ssm, state_ret, state_pool,
           w_ada, b_ada, norm1_g, norm2_g, w_in, w_out, conv_w, conv_b, conv_ln_g, conv_ln_b,
           ssm_conv_w, ssm_conv_b, ssm_dt_bias, ssm_a_log, ssm_d, ssm_norm_g, pool_w, pool_scale,
           router_w, router_b, moe_w_gu, moe_b_gu, moe_w_down, moe_b_down, final_norm_g):
    depth = w_in.shape[0]
    d = x_prompt.shape[-1]
    bp, lp, _ = x_prompt.shape
    bs, ls, _ = x_sample.shape
    n_experts = router_w.shape[-1]
    wm = W_MIX
    hp = SSM_HEADS * SSM_HEADDIM
    hk = RET_HEADS * RET_KDIM

    cb = -(-(bp + bs) // 8) * 8
    c_all = jnp.concatenate([c_prompt, c_sample, jnp.zeros((cb - bp - bs, d), F32)], axis=0)
    mod_all = _ada(c_all, w_ada, b_ada)

    w_in_parts = _prep_w_in(w_in)
    w_out_b = w_out.astype(BF16)
    pool_w_b = pool_w.astype(BF16)
    r3 = lambda a: a.reshape(depth, 1, a.shape[-1])
    n1g, n2g = r3(norm1_g), r3(norm2_g)
    conv_b3, ln_g3, ln_b3 = r3(conv_b), r3(conv_ln_g), r3(conv_ln_b)
    scb3 = r3(ssm_conv_b)
    dtb3 = r3(_pad_lanes(ssm_dt_bias, DT_PAD))
    alog3 = r3(_pad_lanes(ssm_a_log, DT_PAD))
    dsk3 = r3(jnp.repeat(ssm_d, SSM_HEADDIM, axis=-1))
    sng3 = r3(ssm_norm_g)
    psc3 = r3(pool_scale)
    rb3 = r3(router_b)
    rw_hi = router_w.astype(BF16)
    rw_lo = (router_w - rw_hi.astype(F32)).astype(BF16)
    t_all = bp * lp + bs * ls
    expander = (jnp.arange(DT_PAD)[:, None] == (jnp.arange(hp) // SSM_HEADDIM)[None, :]).astype(F32)

    groups = []
    for name, x, b, l, pos0 in (("p", x_prompt, bp, lp, 0), ("s", x_sample, bs, ls, PAST_LEN)):
        (nb, r), (rm, q) = _group_plan(b, l)
        groups.append(dict(name=name, x=x, b=b, l=l, pos0=pos0, nb=nb, r=r, rm=rm, q=q,
                           tables=_ret_tables(l, q, float(pos0)), new=([], [], [], [], [])))

    for layer in range(depth):
        mods = (mod_all[layer, :bp].reshape(bp, 1, -1), mod_all[layer, bp:bp + bs].reshape(bs, 1, -1))
        staged = []
        h2_all = None
        row0 = 0
        for gi, g in enumerate(groups):
            b, l = g["b"], g["l"]
            if g["name"] == "p":
                st_conv = jnp.zeros((b, CONV_HIST, wm), F32)
                st_sconv = jnp.zeros((b, SSD_HIST, SSM_XBC), F32)
                st_ssm = jnp.zeros((b, hp, SSM_STATE), F32)
                st_ret = jnp.zeros((b, hk, RET_VDIM), F32)
                st_pool = jnp.zeros((b, POOL_HIST, wm), F32)
            else:
                st_conv = jnp.pad(state_conv[layer], ((0, 0), (CONV_HIST - (CONV_K - 1), 0), (0, 0)))
                st_sconv = jnp.pad(state_ssm_conv[layer], ((0, 0), (SSD_HIST - (SSM_CONV_K - 1), 0), (0, 0)))
                st_ssm = state_ssm[layer].reshape(b, hp, SSM_STATE)
                st_ret = state_ret[layer].reshape(b, hk, RET_VDIM)
                st_pool = jnp.pad(state_pool[layer], ((0, 0), (POOL_HIST - POOL_BUF, 0), (0, 0)))
            mod = mods[gi]
            uc, us, ur, up = _inproj(g["x"], mod, n1g, w_in_parts, layer, g["nb"], g["r"])
            ya, n_conv = _conv_mixer(uc, st_conv, conv_w, conv_b3, ln_g3, ln_b3, layer, g["rm"])
            yb, n_ssm = _ssd_mixer(us, st_sconv, st_ssm, ssm_conv_w, scb3, dtb3, alog3, dsk3, sng3, expander,
                                   layer, g["rm"], g["q"])
            yc, n_ret = _ret_mixer(ur, st_ret, g["tables"], g["rm"], g["q"])
            yd = _pool_mixer(up, st_pool, pool_w_b, psc3, layer, g["rm"], g["pos0"])
            x1, h2_all, idx, gates = _outproj(g["x"], (ya, yb, yc, yd), mod, n2g, w_out_b, rw_hi, rw_lo, rb3,
                                              layer, g["nb"], g["r"], h2_all, t_all, row0)
            row0 += b * l
            new = g["new"]
            new[0].append(n_conv[:, CONV_HIST - (CONV_K - 1):, :])
            new[1].append(us[:, l - (SSM_CONV_K - 1):, wm:wm + SSM_XBC])
            new[2].append(n_ssm.reshape(b, SSM_HEADS, SSM_HEADDIM, SSM_STATE))
            new[3].append(n_ret.reshape(b, RET_HEADS, RET_KDIM, RET_VDIM))
            new[4].append(up[:, l - POOL_BUF:, :])
            staged.append((x1, idx, gates, mod))

        idx_all = jnp.concatenate([s[1].reshape(-1, TOP_K) for s in staged], axis=0)
        dest, row_token, sb_e, sb_nv, sb_blk = _route(idx_all, n_experts)
        spread, gather_prio, combine_prio = MOE_DMA_PLAN[layer % len(MOE_DMA_PLAN)]
        ybuf = _moe_experts(h2_all, row_token, sb_e, sb_nv, sb_blk, moe_w_gu, moe_b_gu, moe_w_down, moe_b_down, layer,
                            spread, gather_prio)
        off = 0
        for g, (x1, idx, gates, mod) in zip(groups, staged):
            cnt = g["b"] * g["l"] * TOP_K
            g["x"] = _moe_combine(x1, mod, gates, dest[off:off + cnt], ybuf, final_norm_g.reshape(1, d),
                                  layer == depth - 1, combine_prio, g["nb"], g["r"])
            off += cnt

    outs = [g["x"] for g in groups]
    states = []
    for g in groups:
        states.extend(jnp.stack(v, axis=0) for v in g["new"])
    return (outs[0], outs[1], *states)
```

```python
import functools
import math

import numpy as np
import jax
import jax.numpy as jnp
from jax import lax
from jax.experimental import pallas as pl
from jax.experimental.pallas import tpu as pltpu

F32 = jnp.float32
BF16 = jnp.bfloat16
HI = lax.Precision.HIGHEST

EPS = 1e-6
CHUNK = 64
PAST_LEN = 1024
CONV_K = 31
SSM_CONV_K = 4
SSM_HEADS = 8
SSM_HEADDIM = 64
SSM_GROUPS = 2
SSM_STATE = 128
RET_HEADS = 4
RET_KDIM = 64
RET_VDIM = 128
ROPE_BASE = 10000.0
POOL_WINDOWS = (2, 4, 8, 16)
POOL_BUF = 15
TOP_K = 4
SWIGLU_LIMIT = 7.0
SWIGLU_ALPHA = 1.702

W_MIX = 512
LANES = 128
SUBLANES = 8
DT_PAD = LANES
VMEM_LIMIT = 56 * 1024 * 1024

MOE_SUPER = 1024
MOE_SUB = 256
MOE_TF = 256


def _cparams(sem, vmem=VMEM_LIMIT):
    return pltpu.CompilerParams(dimension_semantics=sem, vmem_limit_bytes=vmem)


def _silu(x):
    return x * jax.nn.sigmoid(x)


def _ada_body(c_ref, w_ref, b_ref, o_ref):
    c = c_ref[...]
    o_ref[...] = jnp.dot(_silu(c), w_ref[...], preferred_element_type=F32, precision=HI) + b_ref[...]


def _ada(c_all, w_ada, b_ada):
    nl, d, n6 = w_ada.shape
    cb = c_all.shape[0]
    tn = 1024
    return pl.pallas_call(
        _ada_body,
        out_shape=jax.ShapeDtypeStruct((nl, cb, n6), F32),
        grid=(nl, n6 // tn),
        in_specs=[pl.BlockSpec((cb, d), lambda l, j: (0, 0)),
                  pl.BlockSpec((None, d, tn), lambda l, j: (l, 0, j)),
                  pl.BlockSpec((None, 1, tn), lambda l, j: (l, 0, j))],
        out_specs=pl.BlockSpec((None, cb, tn), lambda l, j: (l, 0, j)),
        compiler_params=_cparams(("arbitrary", "arbitrary")),
        name="ada",
    )(c_all, w_ada, b_ada.reshape(nl, 1, n6))


def _inproj_body(x_ref, sh_ref, sc_ref, g_ref, wc_ref, ws_ref, wr_ref, wp_ref, uc_ref, us_ref, ur_ref, up_ref):
    nb, r, d = x_ref.shape
    x = x_ref[...]
    ms = jnp.mean(x * x, axis=-1, keepdims=True)
    h = x * lax.rsqrt(ms + EPS) * g_ref[...]
    h = h * (1.0 + sc_ref[...]) + sh_ref[...]
    hb = h.reshape(nb * r, d).astype(BF16)
    for w_ref, u_ref in ((wc_ref, uc_ref), (ws_ref, us_ref), (wr_ref, ur_ref), (wp_ref, up_ref)):
        u = jnp.dot(hb, w_ref[...], preferred_element_type=F32)
        u_ref[...] = u.reshape(nb, r, u.shape[-1])


def _inproj(x, mod, norm_g, ws, layer, nb, r):
    b, l, d = x.shape
    grid = (b // nb, l // r)
    xs = pl.BlockSpec((nb, r, d), lambda i, j: (i, j, 0))
    wspecs = [pl.BlockSpec((None, d, w.shape[-1]), lambda i, j: (layer, 0, 0), pipeline_mode=pl.Buffered(1)) for w in ws]
    return pl.pallas_call(
        _inproj_body,
        out_shape=[jax.ShapeDtypeStruct((b, l, w.shape[-1]), F32) for w in ws],
        grid=grid,
        in_specs=[xs,
                  pl.BlockSpec((nb, 1, d), lambda i, j: (i, 0, 0)),
                  pl.BlockSpec((nb, 1, d), lambda i, j: (i, 0, 1)),
                  pl.BlockSpec((None, 1, d), lambda i, j: (layer, 0, 0))] + wspecs,
        out_specs=[pl.BlockSpec((nb, r, w.shape[-1]), lambda i, j: (i, j, 0)) for w in ws],
        compiler_params=_cparams(("arbitrary", "arbitrary")),
        name="inproj",
    )(x, mod, mod, norm_g, *ws)


CONV_HIST = 32
CONV_SUB = 64


def _conv_body(u_ref, st_ref, w_ref, b_ref, lg_ref, lb_ref, y_ref, ns_ref, hist_ref, win_ref):
    r = u_ref.shape[0]
    wm = W_MIX

    @pl.when(pl.program_id(1) == 0)
    def _():
        hist_ref[0:CONV_HIST, :] = st_ref[...]

    u = u_ref[...]
    hist_ref[CONV_HIST:CONV_HIST + r, :] = u[:, :wm] * jax.nn.sigmoid(u[:, wm:])
    sub = min(CONV_SUB, r)
    for r0 in range(0, r, sub):
        acc = jnp.zeros((sub, wm), F32)
        for s in range(SUBLANES):
            a_max = (CONV_K - 1 - s) // SUBLANES
            lo = CONV_HIST + r0 - s - SUBLANES * a_max
            rows = sub + SUBLANES * a_max
            win_ref[0:rows, :] = hist_ref[lo:lo + rows, :]
            for a in range(a_max + 1):
                j = s + SUBLANES * a
                off = SUBLANES * (a_max - a)
                acc = acc + w_ref[CONV_K - 1 - j:CONV_K - j, :] * win_ref[off:off + sub, :]
        acc = acc + b_ref[...]
        mu = jnp.mean(acc, axis=-1, keepdims=True)
        xc = acc - mu
        var = jnp.mean(xc * xc, axis=-1, keepdims=True)
        yn = xc * lax.rsqrt(var + EPS) * lg_ref[...] + lb_ref[...]
        y_ref[r0:r0 + sub, :] = _silu(yn).astype(y_ref.dtype)
    last = hist_ref[r:r + CONV_HIST, :]
    ns_ref[...] = last
    hist_ref[0:CONV_HIST, :] = last


def _conv_mixer(u, st, conv_w, conv_b, ln_g, ln_b, layer, r):
    b, l, _ = u.shape
    wm = W_MIX
    par = lambda k: pl.BlockSpec((None, k, wm), lambda i, j: (layer, 0, 0))
    return pl.pallas_call(
        _conv_body,
        out_shape=[jax.ShapeDtypeStruct((b, l, wm), BF16), jax.ShapeDtypeStruct((b, CONV_HIST, wm), F32)],
        grid=(b, l // r),
        in_specs=[pl.BlockSpec((None, r, 2 * wm), lambda i, j: (i, j, 0)),
                  pl.BlockSpec((None, CONV_HIST, wm), lambda i, j: (i, 0, 0)),
                  par(CONV_K), par(1), par(1), par(1)],
        out_specs=[pl.BlockSpec((None, r, wm), lambda i, j: (i, j, 0)),
                   pl.BlockSpec((None, CONV_HIST, wm), lambda i, j: (i, 0, 0))],
        scratch_shapes=[pltpu.VMEM((r + CONV_HIST, wm), F32),
                        pltpu.VMEM((min(CONV_SUB, r) + CONV_HIST, wm), F32)],
        compiler_params=_cparams(("arbitrary", "arbitrary")),
        name="conv_mixer",
    )(u, st, conv_w, conv_b, ln_g, ln_b)


POOL_HIST = 16


def _pool_body(p_ref, st_ref, w_ref, sc_ref, y_ref, hist_ref, *, pos0):
    r = p_ref.shape[0]
    i = pl.program_id(1)

    @pl.when(i == 0)
    def _():
        hist_ref[0:POOL_HIST, :] = st_ref[...]

    p = p_ref[...]
    hist_ref[POOL_HIST:POOL_HIST + r, :] = p
    pos = pos0 + i * r + lax.broadcasted_iota(jnp.int32, (r, 1), 0)
    outs = []
    for gi, w in enumerate(POOL_WINDOWS):
        lo = gi * LANES
        s = p[:, lo:lo + LANES]
        for j in range(1, w):
            s = s + hist_ref[POOL_HIST - j:POOL_HIST - j + r, lo:lo + LANES]
        cnt = jnp.minimum(pos + 1, w).astype(F32)
        pooled = s / cnt - p[:, lo:lo + LANES]
        outs.append(jnp.dot(pooled.astype(BF16), w_ref[gi], preferred_element_type=F32))
    y = jnp.concatenate(outs, axis=-1) * sc_ref[...]
    y_ref[...] = y.astype(y_ref.dtype)
    hist_ref[0:POOL_HIST, :] = hist_ref[r:r + POOL_HIST, :]


def _pool_mixer(p, st, pool_w, pool_scale, layer, r, pos0):
    b, l, wm = p.shape
    return pl.pallas_call(
        functools.partial(_pool_body, pos0=pos0),
        out_shape=jax.ShapeDtypeStruct((b, l, wm), BF16),
        grid=(b, l // r),
        in_specs=[pl.BlockSpec((None, r, wm), lambda i, j: (i, j, 0)),
                  pl.BlockSpec((None, POOL_HIST, wm), lambda i, j: (i, 0, 0)),
                  pl.BlockSpec((None, len(POOL_WINDOWS), LANES, LANES), lambda i, j: (layer, 0, 0, 0)),
                  pl.BlockSpec((None, 1, wm), lambda i, j: (layer, 0, 0))],
        out_specs=pl.BlockSpec((None, r, wm), lambda i, j: (i, j, 0)),
        scratch_shapes=[pltpu.VMEM((r + POOL_HIST, wm), F32)],
        compiler_params=_cparams(("arbitrary", "arbitrary")),
        name="pool_mixer",
    )(p, st, pool_w, pool_scale)


SSD_HIST = 8
SSM_XBC = W_MIX + 2 * SSM_GROUPS * SSM_STATE
GROUP_W = W_MIX // SSM_GROUPS


def _ssd_body(u_ref, cst_ref, hst_ref, cw_ref, cb_ref, dtb_ref, alog_ref, dsk_ref, ng_ref, ex_ref,
              y_ref, nh_ref, hist_ref, st_ref, *, q):
    r = u_ref.shape[0]
    wm = W_MIX
    n = SSM_STATE
    i = pl.program_id(1)

    @pl.when(i == 0)
    def _():
        hist_ref[0:SSD_HIST, :] = cst_ref[...]
        st_ref[...] = hst_ref[...].T

    hist_ref[SSD_HIST:SSD_HIST + r, :] = u_ref[:, wm:wm + SSM_XBC]
    a_neg = -jnp.exp(alog_ref[...])
    ex = ex_ref[...]
    row = lax.broadcasted_iota(jnp.int32, (q, q), 0)
    col = lax.broadcasted_iota(jnp.int32, (q, q), 1)
    tril = row >= col
    ltri = tril.astype(F32)
    eye = (lax.broadcasted_iota(jnp.int32, (LANES, LANES), 0)
           == lax.broadcasted_iota(jnp.int32, (LANES, LANES), 1)).astype(F32)
    head_of_lane = lax.broadcasted_iota(jnp.int32, (q, GROUP_W), 1) // SSM_HEADDIM
    heads_per_group = SSM_HEADS // SSM_GROUPS

    for c in range(r // q):
        r0 = c * q
        xc = jnp.zeros((q, SSM_XBC), F32)
        for k in range(SSM_CONV_K):
            lo = SSD_HIST + r0 + k - (SSM_CONV_K - 1)
            xc = xc + cw_ref[k:k + 1, :] * hist_ref[lo:lo + q, :]
        xc = _silu(xc + cb_ref[...])
        xs = xc[:, :wm]
        bm = xc[:, wm:wm + SSM_GROUPS * n]
        cm = xc[:, wm + SSM_GROUPS * n:]
        z = u_ref[r0:r0 + q, 0:wm]
        dt = jax.nn.softplus(u_ref[r0:r0 + q, wm + SSM_XBC:wm + SSM_XBC + DT_PAD] + dtb_ref[...])
        cum = jnp.dot(ltri, dt * a_neg, preferred_element_type=F32, precision=HI)
        cum_t = lax.dot_general(eye, cum, (((1,), (1,)), ((), ())), preferred_element_type=F32, precision=HI)
        dt_full = jnp.dot(dt, ex, preferred_element_type=F32, precision=HI)
        cum_full = jnp.dot(cum, ex, preferred_element_type=F32, precision=HI)
        cum_last = cum_full[q - 1:q, :]
        xdt = xs * dt_full
        xw = xdt * jnp.exp(cum_last - cum_full)
        ecum = jnp.exp(cum_full)
        ys = []
        for g in range(SSM_GROUPS):
            gl = slice(g * GROUP_W, (g + 1) * GROUP_W)
            bg = bm[:, g * n:(g + 1) * n].astype(BF16)
            cg = cm[:, g * n:(g + 1) * n].astype(BF16)
            cb = lax.dot_general(cg, bg, (((1,), (1,)), ((), ())), preferred_element_type=F32)
            xdt_g = xdt[:, gl].astype(BF16)
            yg = jnp.zeros((q, GROUP_W), F32)
            for hh in range(heads_per_group):
                h = g * heads_per_group + hh
                seg = cum_full[:, h * SSM_HEADDIM:h * SSM_HEADDIM + 1] - cum_t[h:h + 1, :]
                dec = jnp.exp(jnp.where(tril, seg, -jnp.inf))
                ph = jnp.dot((cb * dec).astype(BF16), xdt_g, preferred_element_type=F32)
                yg = jnp.where(head_of_lane == hh, ph, yg)
            st_g = st_ref[:, gl]
            yg = yg + ecum[:, gl] * jnp.dot(cg, st_g.astype(BF16), preferred_element_type=F32)
            upd = lax.dot_general(bg, xw[:, gl].astype(BF16), (((0,), (0,)), ((), ())), preferred_element_type=F32)
            st_ref[:, gl] = st_g * jnp.exp(cum_last[:, gl]) + upd
            ys.append(yg)
        y = jnp.concatenate(ys, axis=-1) + xs * dsk_ref[...]
        y = y * _silu(z)
        y = y * lax.rsqrt(jnp.mean(y * y, axis=-1, keepdims=True) + EPS) * ng_ref[...]
        y_ref[r0:r0 + q, :] = y.astype(y_ref.dtype)

    hist_ref[0:SSD_HIST, :] = hist_ref[r:r + SSD_HIST, :]

    @pl.when(i == pl.num_programs(1) - 1)
    def _():
        nh_ref[...] = st_ref[...].T


def _ssd_mixer(u, cst, hst, cw, cb, dtb, alog, dsk, ng, ex, layer, r, q):
    b, l, wu = u.shape
    wm = W_MIX
    hp = SSM_HEADS * SSM_HEADDIM
    par = lambda k, w: pl.BlockSpec((None, k, w), lambda i, j: (layer, 0, 0))
    return pl.pallas_call(
        functools.partial(_ssd_body, q=q),
        out_shape=[jax.ShapeDtypeStruct((b, l, wm), BF16), jax.ShapeDtypeStruct((b, hp, SSM_STATE), F32)],
        grid=(b, l // r),
        in_specs=[pl.BlockSpec((None, r, wu), lambda i, j: (i, j, 0)),
                  pl.BlockSpec((None, SSD_HIST, SSM_XBC), lambda i, j: (i, 0, 0)),
                  pl.BlockSpec((None, hp, SSM_STATE), lambda i, j: (i, 0, 0)),
                  par(SSM_CONV_K, SSM_XBC), par(1, SSM_XBC), par(1, DT_PAD), par(1, DT_PAD), par(1, wm), par(1, wm),
                  pl.BlockSpec((DT_PAD, wm), lambda i, j: (0, 0))],
        out_specs=[pl.BlockSpec((None, r, wm), lambda i, j: (i, j, 0)),
                   pl.BlockSpec((None, hp, SSM_STATE), lambda i, j: (i, 0, 0))],
        scratch_shapes=[pltpu.VMEM((r + SSD_HIST, SSM_XBC), F32), pltpu.VMEM((SSM_STATE, hp), F32)],
        compiler_params=_cparams(("arbitrary", "arbitrary")),
        name="ssd_mixer",
    )(u, cst, hst, cw, cb, dtb, alog, dsk, ng, ex)


RET_QK = RET_HEADS * RET_KDIM
RET_HALF = RET_KDIM // 2


def _ret_body(u_ref, st_ref, cos_ref, sin_ref, intra_ref, kdec_ref, qdec_ref, cdec_ref, y_ref, ns_ref, s_ref, *, q):
    r = u_ref.shape[0]
    wm = W_MIX
    i = pl.program_id(1)
    dv = RET_VDIM

    @pl.when(i == 0)
    def _():
        s_ref[...] = jnp.zeros_like(s_ref)
        for h in range(RET_HEADS):
            for b in range(2):
                s_ref[LANES * b + RET_HALF * h:LANES * b + RET_HALF * (h + 1), dv * h:dv * (h + 1)] = (
                    st_ref[RET_KDIM * h + RET_HALF * b:RET_KDIM * h + RET_HALF * (b + 1), :])

    head_of_qlane = (lax.broadcasted_iota(jnp.int32, (q, RET_QK), 1) % LANES) // RET_HALF
    bd_mask = ((lax.broadcasted_iota(jnp.int32, (RET_QK, wm), 0) % LANES) // RET_HALF
               == lax.broadcasted_iota(jnp.int32, (RET_QK, wm), 1) // dv)

    for c in range(r // q):
        r0 = c * q
        cos = cos_ref[r0:r0 + q, :]
        sin = sin_ref[r0:r0 + q, :]
        q1 = u_ref[r0:r0 + q, 0:LANES]
        q2 = u_ref[r0:r0 + q, LANES:2 * LANES]
        k1 = u_ref[r0:r0 + q, RET_QK:RET_QK + LANES]
        k2 = u_ref[r0:r0 + q, RET_QK + LANES:2 * RET_QK]
        v = u_ref[r0:r0 + q, 2 * RET_QK:2 * RET_QK + wm]
        gate = u_ref[r0:r0 + q, 2 * RET_QK + wm:2 * RET_QK + 2 * wm]
        qr = jnp.concatenate([q1 * cos - q2 * sin, q1 * sin + q2 * cos], axis=-1)
        kr = jnp.concatenate([k1 * cos - k2 * sin, k1 * sin + k2 * cos], axis=-1) * (RET_KDIM ** -0.5)
        qrb = qr.astype(BF16)
        krb = kr.astype(BF16)
        vb = v.astype(BF16)
        outs = []
        for h in range(RET_HEADS):
            qm = jnp.where(head_of_qlane == h, qr, 0.0).astype(BF16)
            sc = lax.dot_general(qm, krb, (((1,), (1,)), ((), ())), preferred_element_type=F32)
            ph = (sc * intra_ref[h]).astype(BF16)
            outs.append(jnp.dot(ph, vb[:, dv * h:dv * (h + 1)], preferred_element_type=F32))
        o = jnp.concatenate(outs, axis=-1)
        s_old = s_ref[...]
        o = o + qdec_ref[...] * jnp.dot(qrb, s_old.astype(BF16), preferred_element_type=F32)
        kv = lax.dot_general(krb, (v * kdec_ref[...]).astype(BF16), (((0,), (0,)), ((), ())), preferred_element_type=F32)
        s_ref[...] = s_old * cdec_ref[...] + jnp.where(bd_mask, kv, 0.0)
        normed = []
        for h in range(RET_HEADS):
            oh = o[:, dv * h:dv * (h + 1)]
            normed.append(oh * lax.rsqrt(jnp.mean(oh * oh, axis=-1, keepdims=True) + EPS))
        o = jnp.concatenate(normed, axis=-1) * _silu(gate)
        y_ref[r0:r0 + q, :] = o.astype(y_ref.dtype)

    @pl.when(i == pl.num_programs(1) - 1)
    def _():
        for h in range(RET_HEADS):
            for b in range(2):
                ns_ref[RET_KDIM * h + RET_HALF * b:RET_KDIM * h + RET_HALF * (b + 1), :] = (
                    s_ref[LANES * b + RET_HALF * h:LANES * b + RET_HALF * (h + 1), dv * h:dv * (h + 1)])


def _ret_tables(l, q, pos0):
    inv = 1.0 / (ROPE_BASE ** jnp.linspace(0.0, 1.0, RET_HALF, dtype=F32))
    pos = pos0 + jnp.arange(l, dtype=F32)
    ang = pos[:, None] * inv[None, :]
    cos = jnp.tile(jnp.cos(ang), (1, RET_HEADS))
    sin = jnp.tile(jnp.sin(ang), (1, RET_HEADS))
    log_gamma = jnp.log(1.0 - 2.0 ** (-5.0 - jnp.arange(RET_HEADS, dtype=F32)))
    idx = jnp.arange(q, dtype=F32)
    intra = jnp.exp(jnp.abs(idx[:, None] - idx[None, :])[None, :, :] * log_gamma[:, None, None])
    kdec = jnp.repeat(jnp.exp((q - idx)[:, None] * log_gamma), RET_VDIM, axis=1)
    qdec = jnp.repeat(jnp.exp(idx[:, None] * log_gamma), RET_VDIM, axis=1)
    cdec = jnp.repeat(jnp.exp(q * log_gamma), RET_VDIM)[None, :]
    return cos, sin, intra, kdec, qdec, cdec


def _ret_mixer(u, st, tables, r, q):
    b, l, wu = u.shape
    wm = W_MIX
    cos, sin, intra, kdec, qdec, cdec = tables
    hk = RET_HEADS * RET_KDIM
    full = lambda a: pl.BlockSpec(a.shape, lambda i, j: (0,) * a.ndim)
    return pl.pallas_call(
        functools.partial(_ret_body, q=q),
        out_shape=[jax.ShapeDtypeStruct((b, l, wm), BF16), jax.ShapeDtypeStruct((b, hk, RET_VDIM), F32)],
        grid=(b, l // r),
        in_specs=[pl.BlockSpec((None, r, wu), lambda i, j: (i, j, 0)),
                  pl.BlockSpec((None, hk, RET_VDIM), lambda i, j: (i, 0, 0)),
                  pl.BlockSpec((r, LANES), lambda i, j: (j, 0)),
                  pl.BlockSpec((r, LANES), lambda i, j: (j, 0)),
                  full(intra), full(kdec), full(qdec), full(cdec)],
        out_specs=[pl.BlockSpec((None, r, wm), lambda i, j: (i, j, 0)),
                   pl.BlockSpec((None, hk, RET_VDIM), lambda i, j: (i, 0, 0))],
        scratch_shapes=[pltpu.VMEM((RET_QK, wm), F32)],
        compiler_params=_cparams(("arbitrary", "arbitrary")),
        name="ret_mixer",
    )(u, st, cos, sin, intra, kdec, qdec, cdec)


def _outproj_body(x_ref, ya_ref, yb_ref, yc_ref, yd_ref, g1_ref, sh_ref, sc_ref, ng_ref, wo_ref, rwh_ref, rwl_ref,
                  rb_ref, *rest):
    x1_ref, h2_ref, idx_ref, gate_ref = rest[-4:]
    nb, r, d = x_ref.shape
    n = nb * r
    ycat = jnp.concatenate([ya_ref[...], yb_ref[...], yc_ref[...], yd_ref[...]], axis=-1).reshape(n, d)
    y = jnp.dot(ycat, wo_ref[...], preferred_element_type=F32).reshape(nb, r, d)
    x1 = x_ref[...] + g1_ref[...] * y
    x1_ref[...] = x1
    ms = jnp.mean(x1 * x1, axis=-1, keepdims=True)
    h2 = x1 * lax.rsqrt(ms + EPS) * ng_ref[...]
    h2 = (h2 * (1.0 + sc_ref[...]) + sh_ref[...]).reshape(n, d)
    h2_ref[...] = h2
    h2h = h2.astype(BF16)
    h2l = (h2 - h2h.astype(F32)).astype(BF16)
    rwh = rwh_ref[...]
    logits = (jnp.dot(h2h, rwh, preferred_element_type=F32) + jnp.dot(h2l, rwh, preferred_element_type=F32)
              + jnp.dot(h2h, rwl_ref[...], preferred_element_type=F32) + rb_ref[...])
    ne = logits.shape[-1]
    lane = lax.broadcasted_iota(jnp.int32, (n, ne), 1)
    vals, idxs = [], []
    for _ in range(TOP_K):
        m = jnp.max(logits, axis=-1, keepdims=True)
        idx = jnp.min(jnp.where(logits == m, lane, ne), axis=-1, keepdims=True)
        vals.append(m)
        idxs.append(idx)
        logits = jnp.where(lane == idx, -jnp.inf, logits)
    v = jnp.concatenate(vals, axis=-1)
    ex = jnp.exp(v - v[:, 0:1])
    gate_ref[...] = (ex / jnp.sum(ex, axis=-1, keepdims=True)).reshape(nb, r, TOP_K)
    idx_ref[...] = jnp.concatenate(idxs, axis=-1).reshape(nb, r, TOP_K)


def _outproj(x, ys, mod, norm_g, w_out, rw_hi, rw_lo, router_b, layer, nb, r, h2_shared, t_all, row0):
    b, l, d = x.shape
    ne = rw_hi.shape[-1]
    n = nb * r
    lt = l // r
    blk0 = row0 // n
    blk = lambda w: pl.BlockSpec((nb, r, w), lambda i, j: (i, j, 0))
    modspec = lambda k: pl.BlockSpec((nb, 1, d), lambda i, j: (i, 0, k))
    in_specs = [blk(d), blk(W_MIX), blk(W_MIX), blk(W_MIX), blk(W_MIX),
                modspec(2), modspec(3), modspec(4),
                pl.BlockSpec((None, 1, d), lambda i, j: (layer, 0, 0)),
                pl.BlockSpec((None, d, d), lambda i, j: (layer, 0, 0), pipeline_mode=pl.Buffered(1)),
                pl.BlockSpec((None, d, ne), lambda i, j: (layer, 0, 0)),
                pl.BlockSpec((None, d, ne), lambda i, j: (layer, 0, 0)),
                pl.BlockSpec((None, 1, ne), lambda i, j: (layer, 0, 0))]
    args = [x, *ys, mod, mod, mod, norm_g, w_out, rw_hi, rw_lo, router_b]
    aliases = {}
    if h2_shared is not None:
        in_specs.append(pl.BlockSpec(memory_space=pl.ANY))
        args.append(h2_shared)
        aliases = {len(args) - 1: 1}
    return pl.pallas_call(
        _outproj_body,
        out_shape=[jax.ShapeDtypeStruct((b, l, d), F32), jax.ShapeDtypeStruct((t_all, d), F32),
                   jax.ShapeDtypeStruct((b, l, TOP_K), jnp.int32), jax.ShapeDtypeStruct((b, l, TOP_K), F32)],
        grid=(b // nb, lt),
        in_specs=in_specs,
        out_specs=[blk(d), pl.BlockSpec((n, d), lambda i, j: (blk0 + i * lt + j, 0)), blk(TOP_K), blk(TOP_K)],
        input_output_aliases=aliases,
        compiler_params=_cparams(("arbitrary", "arbitrary")),
        name="outproj",
    )(*args)


MOE_SUBS = MOE_SUPER // MOE_SUB
MOE_DOWN_COLS = 512
GATHER_UNROLL = 8


def _expert_body(e_ref, nv_ref, blk_ref, tok_ref, tokn_ref, h_ref, wg_ref, wu_ref, bg_ref, bu_ref, wd_ref, bd_ref,
                 o_ref, xg_ref, xb_ref, sem):
    b = pl.program_id(0)
    j = pl.program_id(1)
    nv = nv_ref[b]
    nsub = (nv + MOE_SUB - 1) // MOE_SUB

    def row_copy(tref, r):
        return pltpu.make_async_copy(h_ref.at[pl.ds(tref[0, r], 1), :], xg_ref.at[pl.ds(r, 1), :], sem)

    def issue_rows(tref, n_rows):
        def body(i, c):
            for k in range(GATHER_UNROLL):
                row_copy(tref, i * GATHER_UNROLL + k).start()
            return c
        lax.fori_loop(0, n_rows // GATHER_UNROLL, body, 0)

    def wait_rows(n_rows):
        pltpu.make_async_copy(h_ref.at[pl.ds(0, n_rows), :], xg_ref.at[pl.ds(0, n_rows), :], sem).wait()

    @pl.when(j == 0)
    def _():
        @pl.when(b == 0)
        def _():
            issue_rows(tok_ref, nsub * MOE_SUB)

        for n in range(1, MOE_SUBS + 1):
            @pl.when(nsub == n)
            def _():
                wait_rows(n * MOE_SUB)

        for s in range(MOE_SUBS):
            rows = slice(s * MOE_SUB, (s + 1) * MOE_SUB)

            @pl.when(s < nsub)
            def _():
                xb_ref[rows, :] = xg_ref[rows, :].astype(BF16)

        nvn = nv_ref[jnp.minimum(b + 1, pl.num_programs(0) - 1)]

        @pl.when(b + 1 < pl.num_programs(0))
        def _():
            issue_rows(tokn_ref, (nvn + MOE_SUB - 1) // MOE_SUB * MOE_SUB)

        @pl.when(nv > 0)
        def _():
            o_ref[...] = jnp.broadcast_to(bd_ref[...], o_ref.shape)

    for n in range(1, MOE_SUBS + 1):
        m = n * MOE_SUB

        @pl.when(nsub == n)
        def _():
            x = xb_ref[0:m, :]
            g = jnp.dot(x, wg_ref[...].astype(BF16), preferred_element_type=F32) + bg_ref[...]
            u = jnp.dot(x, wu_ref[...].astype(BF16), preferred_element_type=F32) + bu_ref[...]
            g = jnp.minimum(g, SWIGLU_LIMIT)
            u = jnp.clip(u, -SWIGLU_LIMIT, SWIGLU_LIMIT)
            act = ((u + 1.0) * (g * jax.nn.sigmoid(SWIGLU_ALPHA * g))).astype(BF16)
            for c0 in range(0, o_ref.shape[-1], MOE_DOWN_COLS):
                cols = slice(c0, c0 + MOE_DOWN_COLS)
                o_ref[0:m, cols] += jnp.dot(act, wd_ref[:, cols].astype(BF16), preferred_element_type=F32)


def _moe_experts(h2, row_token, sb_e, sb_nv, sb_blk, w_gu, b_gu, w_down, b_down, layer):
    t, d = h2.shape
    n_sb = row_token.shape[0]
    n_rows = n_sb * MOE_SUPER
    nl, ne, _, f2 = w_gu.shape
    f = f2 // 2
    tf = min(MOE_TF, f)
    nj = f // tf

    def jj(b, j, nv):
        return jnp.where(nv[b] > 0, j, nj - 1)

    return pl.pallas_call(
        _expert_body,
        out_shape=jax.ShapeDtypeStruct((n_rows, d), F32),
        grid_spec=pltpu.PrefetchScalarGridSpec(
            num_scalar_prefetch=3,
            grid=(n_sb, nj),
            in_specs=[
                pl.BlockSpec((None, 1, MOE_SUPER), lambda b, j, e, nv, blk: (b, 0, 0), memory_space=pltpu.SMEM),
                pl.BlockSpec((None, 1, MOE_SUPER), lambda b, j, e, nv, blk: (jnp.minimum(b + 1, n_sb - 1), 0, 0),
                             memory_space=pltpu.SMEM),
                pl.BlockSpec(memory_space=pl.ANY),
                pl.BlockSpec((None, None, d, tf), lambda b, j, e, nv, blk: (layer, e[b], 0, jj(b, j, nv))),
                pl.BlockSpec((None, None, d, tf), lambda b, j, e, nv, blk: (layer, e[b], 0, nj + jj(b, j, nv))),
                pl.BlockSpec((None, None, 1, tf), lambda b, j, e, nv, blk: (layer, e[b], 0, jj(b, j, nv))),
                pl.BlockSpec((None, None, 1, tf), lambda b, j, e, nv, blk: (layer, e[b], 0, nj + jj(b, j, nv))),
                pl.BlockSpec((None, None, tf, d), lambda b, j, e, nv, blk: (layer, e[b], jj(b, j, nv), 0)),
                pl.BlockSpec((None, None, 1, d), lambda b, j, e, nv, blk: (layer, e[b], 0, 0)),
            ],
            out_specs=pl.BlockSpec((MOE_SUPER, d), lambda b, j, e, nv, blk: (blk[b], 0)),
            scratch_shapes=[pltpu.VMEM((MOE_SUPER, d), F32), pltpu.VMEM((MOE_SUPER, d), BF16),
                            pltpu.SemaphoreType.DMA(())]),
        compiler_params=_cparams(("arbitrary", "arbitrary")),
        name="moe_experts",
    )(sb_e, sb_nv, sb_blk, row_token, row_token, h2, w_gu, w_gu, b_gu.reshape(nl, ne, 1, f2),
      b_gu.reshape(nl, ne, 1, f2), w_down, b_down.reshape(nl, ne, 1, d))


def _combine_body(dest_ref, x_ref, g2_ref, gate_ref, fg_ref, y_ref, o_ref, gbuf_ref, sem, *, final):
    nb, r, d = x_ref.shape
    n = nb * r

    def issue(i, c):
        for tt in range(GATHER_UNROLL):
            t = i * GATHER_UNROLL + tt
            for k in range(TOP_K):
                pltpu.make_async_copy(y_ref.at[pl.ds(dest_ref[0, t * TOP_K + k], 1), :],
                                      gbuf_ref.at[pl.ds(k * n + t, 1), :], sem).start()
        return c

    lax.fori_loop(0, n // GATHER_UNROLL, issue, 0)
    pltpu.make_async_copy(y_ref.at[pl.ds(0, TOP_K * n), :], gbuf_ref, sem).wait()
    gate = gate_ref[...].reshape(n, TOP_K)
    f = gbuf_ref[0:n, :] * gate[:, 0:1]
    for k in range(1, TOP_K):
        f = f + gbuf_ref[k * n:(k + 1) * n, :] * gate[:, k:k + 1]
    x2 = x_ref[...] + g2_ref[...] * f.reshape(nb, r, d)
    if final:
        x2 = x2 * lax.rsqrt(jnp.mean(x2 * x2, axis=-1, keepdims=True) + EPS) * fg_ref[...]
    o_ref[...] = x2


def _moe_combine(x1, mod, gates, dest, ybuf, final_g, final, nb, r):
    b, l, d = x1.shape
    n = nb * r
    nt = (b // nb) * (l // r)
    lt = l // r
    blk = lambda w: pl.BlockSpec((nb, r, w), lambda i, j: (i, j, 0))
    return pl.pallas_call(
        functools.partial(_combine_body, final=final),
        out_shape=jax.ShapeDtypeStruct((b, l, d), F32),
        grid=(b // nb, lt),
        in_specs=[pl.BlockSpec((None, 1, n * TOP_K), lambda i, j: (i * lt + j, 0, 0), memory_space=pltpu.SMEM),
                  blk(d),
                  pl.BlockSpec((nb, 1, d), lambda i, j: (i, 0, 5)),
                  blk(TOP_K),
                  pl.BlockSpec((1, d), lambda i, j: (0, 0)),
                  pl.BlockSpec(memory_space=pl.ANY)],
        out_specs=blk(d),
        scratch_shapes=[pltpu.VMEM((TOP_K * n, d), F32), pltpu.SemaphoreType.DMA(())],
        compiler_params=_cparams(("arbitrary", "arbitrary")),
        name="moe_combine",
    )(dest.reshape(nt, 1, n * TOP_K), x1, mod, gates, final_g, ybuf)


def _route(idx_all, n_experts):
    m = idx_all.shape[0] * TOP_K
    e_flat = idx_all.reshape(m)
    onehot = (e_flat[:, None] == jnp.arange(n_experts, dtype=jnp.int32)[None, :]).astype(jnp.int32)
    csum = jnp.cumsum(onehot, axis=0)
    counts = csum[-1]
    rank = jnp.sum(csum * onehot, axis=1) - 1
    n_sb_e = (counts + MOE_SUPER - 1) // MOE_SUPER
    sb_end = jnp.cumsum(n_sb_e)
    sb_start = sb_end - n_sb_e
    dest = sb_start[e_flat] * MOE_SUPER + rank
    n_sb = -(-m // MOE_SUPER) + n_experts
    n_rows = n_sb * MOE_SUPER
    n_used = sb_end[-1]
    sb_id = jnp.arange(n_sb, dtype=jnp.int32)
    sb_clamped = jnp.minimum(sb_id, n_used - 1)
    sb_e = jnp.minimum(jnp.searchsorted(sb_end, sb_clamped, side='right'), n_experts - 1).astype(jnp.int32)
    within = sb_clamped - sb_start[sb_e]
    sb_nv = jnp.where(sb_id < n_used, jnp.clip(counts[sb_e] - within * MOE_SUPER, 0, MOE_SUPER), 0).astype(jnp.int32)
    row_token = jnp.zeros((n_rows,), jnp.int32).at[dest].set(jnp.arange(m, dtype=jnp.int32) // TOP_K)
    return (dest.astype(jnp.int32), row_token.reshape(n_sb, 1, MOE_SUPER), sb_e, sb_nv,
            sb_clamped.astype(jnp.int32))


def _prep_w_in(w_in):
    wm = W_MIX
    o = 0
    w_conv = w_in[..., o:o + 2 * wm]
    o += 2 * wm
    n_ssm = 2 * wm + 2 * SSM_GROUPS * SSM_STATE
    w_ssm = w_in[..., o:o + n_ssm]
    w_dt = w_in[..., o + n_ssm:o + n_ssm + SSM_HEADS]
    o += n_ssm + SSM_HEADS
    w_dt = jnp.pad(w_dt, ((0, 0), (0, 0), (0, DT_PAD - SSM_HEADS)))
    w_ssm = jnp.concatenate([w_ssm, w_dt], axis=-1)

    def halves_first(w):
        s = w.shape[:-1]
        return w.reshape(*s, RET_HEADS, 2, RET_HALF).swapaxes(-2, -3).reshape(*s, RET_QK)

    w_q = halves_first(w_in[..., o:o + RET_QK])
    w_k = halves_first(w_in[..., o + RET_QK:o + 2 * RET_QK])
    w_ret = jnp.concatenate([w_q, w_k, w_in[..., o + 2 * RET_QK:o + 2 * RET_QK + 2 * wm]], axis=-1)
    o += 2 * RET_QK + 2 * wm
    w_pool = w_in[..., o:o + wm]
    return [w.astype(BF16) for w in (w_conv, w_ssm, w_ret, w_pool)]


def _pad_lanes(v, width, value=0.0):
    return jnp.pad(v, ((0, 0), (0, width - v.shape[-1])), constant_values=value)


def _group_plan(b, l):
    if l >= 256:
        return (1, 256), (256, CHUNK)
    nb = max(1, min(b, 256 // l))
    return (nb, l), (l, min(CHUNK, l))


def kernel(x_prompt, x_sample, c_prompt, c_sample, state_conv, state_ssm_conv, state_ssm, state_ret, state_pool,
           w_ada, b_ada, norm1_g, norm2_g, w_in, w_out, conv_w, conv_b, conv_ln_g, conv_ln_b,
           ssm_conv_w, ssm_conv_b, ssm_dt_bias, ssm_a_log, ssm_d, ssm_norm_g, pool_w, pool_scale,
           router_w, router_b, moe_w_gu, moe_b_gu, moe_w_down, moe_b_down, final_norm_g):
    depth = w_in.shape[0]
    d = x_prompt.shape[-1]
    bp, lp, _ = x_prompt.shape
    bs, ls, _ = x_sample.shape
    n_experts = router_w.shape[-1]
    wm = W_MIX
    hp = SSM_HEADS * SSM_HEADDIM
    hk = RET_HEADS * RET_KDIM

    cb = -(-(bp + bs) // 8) * 8
    c_all = jnp.concatenate([c_prompt, c_sample, jnp.zeros((cb - bp - bs, d), F32)], axis=0)
    mod_all = _ada(c_all, w_ada, b_ada)

    w_in_parts = _prep_w_in(w_in)
    w_out_b = w_out.astype(BF16)
    pool_w_b = pool_w.astype(BF16)
    r3 = lambda a: a.reshape(depth, 1, a.shape[-1])
    n1g, n2g = r3(norm1_g), r3(norm2_g)
    conv_b3, ln_g3, ln_b3 = r3(conv_b), r3(conv_ln_g), r3(conv_ln_b)
    scb3 = r3(ssm_conv_b)
    dtb3 = r3(_pad_lanes(ssm_dt_bias, DT_PAD))
    alog3 = r3(_pad_lanes(ssm_a_log, DT_PAD))
    dsk3 = r3(jnp.repeat(ssm_d, SSM_HEADDIM, axis=-1))
    sng3 = r3(ssm_norm_g)
    psc3 = r3(pool_scale)
    rb3 = r3(router_b)
    rw_hi = router_w.astype(BF16)
    rw_lo = (router_w - rw_hi.astype(F32)).astype(BF16)
    t_all = bp * lp + bs * ls
    expander = (jnp.arange(DT_PAD)[:, None] == (jnp.arange(hp) // SSM_HEADDIM)[None, :]).astype(F32)

    groups = []
    for name, x, b, l, pos0 in (("p", x_prompt, bp, lp, 0), ("s", x_sample, bs, ls, PAST_LEN)):
        (nb, r), (rm, q) = _group_plan(b, l)
        groups.append(dict(name=name, x=x, b=b, l=l, pos0=pos0, nb=nb, r=r, rm=rm, q=q,
                           tables=_ret_tables(l, q, float(pos0)), new=([], [], [], [], [])))

    for layer in range(depth):
        mods = (mod_all[layer, :bp].reshape(bp, 1, -1), mod_all[layer, bp:bp + bs].reshape(bs, 1, -1))
        staged = []
        h2_all = None
        row0 = 0
        for gi, g in enumerate(groups):
            b, l = g["b"], g["l"]
            if g["name"] == "p":
                st_conv = jnp.zeros((b, CONV_HIST, wm), F32)
                st_sconv = jnp.zeros((b, SSD_HIST, SSM_XBC), F32)
                st_ssm = jnp.zeros((b, hp, SSM_STATE), F32)
                st_ret = jnp.zeros((b, hk, RET_VDIM), F32)
                st_pool = jnp.zeros((b, POOL_HIST, wm), F32)
            else:
                st_conv = jnp.pad(state_conv[layer], ((0, 0), (CONV_HIST - (CONV_K - 1), 0), (0, 0)))
                st_sconv = jnp.pad(state_ssm_conv[layer], ((0, 0), (SSD_HIST - (SSM_CONV_K - 1), 0), (0, 0)))
                st_ssm = state_ssm[layer].reshape(b, hp, SSM_STATE)
                st_ret = state_ret[layer].reshape(b, hk, RET_VDIM)
                st_pool = jnp.pad(state_pool[layer], ((0, 0), (POOL_HIST - POOL_BUF, 0), (0, 0)))
            mod = mods[gi]
            uc, us, ur, up = _inproj(g["x"], mod, n1g, w_in_parts, layer, g["nb"], g["r"])
            ya, n_conv = _conv_mixer(uc, st_conv, conv_w, conv_b3, ln_g3, ln_b3, layer, g["rm"])
            yb, n_ssm = _ssd_mixer(us, st_sconv, st_ssm, ssm_conv_w, scb3, dtb3, alog3, dsk3, sng3, expander,
                                   layer, g["rm"], g["q"])
            yc, n_ret = _ret_mixer(ur, st_ret, g["tables"], g["rm"], g["q"])
            yd = _pool_mixer(up, st_pool, pool_w_b, psc3, layer, g["rm"], g["pos0"])
            x1, h2_all, idx, gates = _outproj(g["x"], (ya, yb, yc, yd), mod, n2g, w_out_b, rw_hi, rw_lo, rb3,
                                              layer, g["nb"], g["r"], h2_all, t_all, row0)
            row0 += b * l
            new = g["new"]
            new[0].append(n_conv[:, CONV_HIST - (CONV_K - 1):, :])
            new[1].append(us[:, l - (SSM_CONV_K - 1):, wm:wm + SSM_XBC])
            new[2].append(n_ssm.reshape(b, SSM_HEADS, SSM_HEADDIM, SSM_STATE))
            new[3].append(n_ret.reshape(b, RET_HEADS, RET_KDIM, RET_VDIM))
            new[4].append(up[:, l - POOL_BUF:, :])
            staged.append((x1, idx, gates, mod))

        idx_all = jnp.concatenate([s[1].reshape(-1, TOP_K) for s in staged], axis=0)
        dest, row_token, sb_e, sb_nv, sb_blk = _route(idx_all, n_experts)
        ybuf = _moe_experts(h2_all, row_token, sb_e, sb_nv, sb_blk, moe_w_gu, moe_b_gu, moe_w_down, moe_b_down, layer)
        off = 0
        for g, (x1, idx, gates, mod) in zip(groups, staged):
            cnt = g["b"] * g["l"] * TOP_K
            g["x"] = _moe_combine(x1, mod, gates, dest[off:off + cnt], ybuf, final_norm_g.reshape(1, d),
                                  layer == depth - 1, g["nb"], g["r"])
            off += cnt

    outs = [g["x"] for g in groups]
    states = []
    for g in groups:
        states.extend(jnp.stack(v, axis=0) for v in g["new"])
    return (outs[0], outs[1], *states)
```
